```python
import math
import jax, jax.numpy as jnp
from jax import lax
import numpy as np

D_MODEL = 1024
BATCH = 4
SEQ = 8192
DEPTH = 4
DEC_BATCH = 8
DEC_SEQ = 32
PAST_LEN = 1024

CHUNK = 64
N_MIXERS = 3
N_SSD_LAYERS = (DEPTH + 2) // 3
N_SB_LAYERS = (DEPTH + 1) // 3
N_SWA_LAYERS = DEPTH // 3
NORM_EPS = 1e-6
FFN_RESID = 0.5
D_FF = ((8 * D_MODEL // 3 + 127) // 128) * 128

SSD_D_INNER = 2 * D_MODEL
SSD_HEAD_DIM = 64
SSD_HEADS = SSD_D_INNER // SSD_HEAD_DIM
SSD_GROUPS = 4
SSD_STATE = 128
SSD_CONV = 4
SSD_GN = SSD_GROUPS * SSD_STATE
SSD_CONV_DIM = SSD_D_INNER + 2 * SSD_GN
SSD_IN_DIM = SSD_D_INNER + SSD_CONV_DIM + SSD_HEADS
SSD_CHUNK = CHUNK

SB_HEAD_DIM = 64
SB_HEADS = D_MODEL // SB_HEAD_DIM
SB_BLOCK = 128

SWA_HEAD_DIM = 64
SWA_Q_HEADS = D_MODEL // SWA_HEAD_DIM
SWA_KV_HEADS = 4
SWA_WINDOW = 128
SWA_BACK_CHUNKS = -(-SWA_WINDOW // CHUNK)
SWA_ROWS = SWA_BACK_CHUNKS * CHUNK

kernel_name = 'hybrid_streaming_ssd_stickbreak_swa_step'


def rmsnorm(x, g):
    x32 = x.astype(jnp.float32)
    y = x32 * lax.rsqrt(jnp.mean(x32 * x32, axis=-1, keepdims=True) + NORM_EPS)
    return (y * g.astype(jnp.float32)).astype(x.dtype)


def adaln_in(x, g, shift, scale):
    return rmsnorm(x, g) * (1 + scale) + shift


def swiglu(h, w_in, w_out):
    a, b = jnp.split(h @ w_in, 2, axis=-1)
    return (jax.nn.silu(a) * b) @ w_out


def causal_dwconv(xpad, w, b):
    y = lax.conv_general_dilated(xpad, w[:, None, :].astype(xpad.dtype), window_strides=(1,), padding='VALID',
                                 dimension_numbers=('NWC', 'WIO', 'NWC'), feature_group_count=xpad.shape[-1])
    return y + b.astype(xpad.dtype)


def ssd_scan(x, dt, A, Bm, Cm, h0, q):
    f32 = jnp.float32
    bsz, L, H, P = x.shape
    G, N = Bm.shape[2], Bm.shape[3]
    R = H // G
    nc = L // q
    x = x.astype(f32).reshape(bsz, nc, q, G, R, P)
    dt = dt.astype(f32).reshape(bsz, nc, q, G, R)
    Bm = Bm.astype(f32).reshape(bsz, nc, q, G, N)
    Cm = Cm.astype(f32).reshape(bsz, nc, q, G, N)
    acum = jnp.cumsum(dt * A.reshape(G, R), axis=2)
    causal = jnp.tril(jnp.ones((q, q), dtype=bool))[None, None, :, :, None, None]
    seg = acum[:, :, :, None] - acum[:, :, None, :]
    decay_ts = jnp.exp(jnp.where(causal, seg, -jnp.inf))
    cb = jnp.einsum('bctgn,bcsgn->bctsg', Cm, Bm)
    y_diag = jnp.einsum('bctsg,bctsgr,bcsgr,bcsgrp->bctgrp', cb, decay_ts, dt, x)
    decay_end = jnp.exp(acum[:, :, -1:] - acum)
    states = jnp.einsum('bcsgn,bcsgr,bcsgrp->bcgrpn', Bm, decay_end * dt, x)
    chunk_decay = jnp.exp(acum[:, :, -1])

    def step(h, inp):
        s_c, d_c = inp
        return h * d_c[..., None, None] + s_c, h

    h_last, h_in = lax.scan(step, h0.astype(f32).reshape(bsz, G, R, P, N),
                            (jnp.moveaxis(states, 1, 0), jnp.moveaxis(chunk_decay, 1, 0)))
    h_in = jnp.moveaxis(h_in, 0, 1)
    y_off = jnp.einsum('bctgn,bcgrpn,bctgr->bctgrp', Cm, h_in, jnp.exp(acum))
    y = (y_diag + y_off).reshape(bsz, L, H, P)
    return y, h_last.reshape(bsz, H, P, N)


def ssd_mixer(h, conv_state, ssm_state, w_in, conv_w, conv_b, dt_bias, a_log, d_skip, norm_g, w_out, chunk):
    f32 = jnp.float32
    bsz, L, _ = h.shape
    proj = h @ w_in
    z = proj[..., :SSD_D_INNER]
    xbc = proj[..., SSD_D_INNER:SSD_D_INNER + SSD_CONV_DIM]
    dt_raw = proj[..., SSD_D_INNER + SSD_CONV_DIM:]
    xpad = jnp.concatenate([conv_state.astype(xbc.dtype), xbc], axis=1)
    new_conv = xpad[:, xpad.shape[1] - (SSD_CONV - 1):]
    xbc = jax.nn.silu(causal_dwconv(xpad, conv_w, conv_b))
    xs = xbc[..., :SSD_D_INNER].reshape(bsz, L, SSD_HEADS, SSD_HEAD_DIM)
    Bm = xbc[..., SSD_D_INNER:SSD_D_INNER + SSD_GN].reshape(bsz, L, SSD_GROUPS, SSD_STATE)
    Cm = xbc[..., SSD_D_INNER + SSD_GN:].reshape(bsz, L, SSD_GROUPS, SSD_STATE)
    dt = jax.nn.softplus(dt_raw.astype(f32) + dt_bias.astype(f32))
    A = -jnp.exp(a_log.astype(f32))
    y, h_new = ssd_scan(xs, dt, A, Bm, Cm, ssm_state, chunk)
    y = y + d_skip.astype(f32)[:, None] * xs.astype(f32)
    y = y.reshape(bsz, L, SSD_D_INNER).astype(h.dtype)
    y = rmsnorm(y * jax.nn.silu(z), norm_g)
    return y @ w_out, new_conv, h_new.astype(ssm_state.dtype)


def sb_piece(q, k, v, q_pos, k_pos, acc):
    z = jnp.einsum('bqhd,bkhd->bhqk', q, k).astype(jnp.float32) * (SB_HEAD_DIM ** -0.5)
    mask = k_pos[None, :] < q_pos[:, None]
    u = jnp.where(mask, jax.nn.log_sigmoid(-z), 0.0)
    later = lax.cumsum(u, axis=3, reverse=True) - u + acc[..., None]
    w = jnp.where(mask, jnp.exp(jax.nn.log_sigmoid(z) + later), 0.0)
    out = jnp.einsum('bhqk,bkhd->bhqd', w, v.astype(jnp.float32))
    return out, acc + jnp.sum(u, axis=-1)


def sb_attention_prompt(q, k, v):
    bsz, L, H, dh = q.shape
    nb = L // SB_BLOCK

    def blocks(t):
        return jnp.swapaxes(t.reshape(bsz, nb, SB_BLOCK, H, dh), 0, 1)

    qb, kb, vb = blocks(q), blocks(k), blocks(v)
    pos = jnp.arange(L, dtype=jnp.int32).reshape(nb, SB_BLOCK)

    def one_query_block(args):
        qi, qpos = args

        def step(carry, kv):
            acc, out = carry
            kj, vj, kpos = kv
            o, acc = sb_piece(qi, kj, vj, qpos, kpos, acc)
            return (acc, out + o), None

        init = (jnp.zeros((bsz, H, SB_BLOCK), jnp.float32), jnp.zeros((bsz, H, SB_BLOCK, dh), jnp.float32))
        (_, out), _ = lax.scan(step, init, (kb, vb, pos), reverse=True)
        return out

    out = lax.map(one_query_block, (qb, pos))
    return jnp.transpose(out, (1, 0, 3, 2, 4)).reshape(bsz, L, H * dh)


def sb_mixer(h, past_k, past_v, w_qkv, w_out):
    bsz, L, _ = h.shape
    q, k, v = [t.reshape(bsz, L, SB_HEADS, SB_HEAD_DIM) for t in jnp.split(h @ w_qkv, 3, axis=-1)]
    if past_k is None:
        o = sb_attention_prompt(q, k, v)
    else:
        past = past_k.shape[1]
        kf = jnp.concatenate([past_k.astype(k.dtype), k], axis=1)
        vf = jnp.concatenate([past_v.astype(v.dtype), v], axis=1)
        q_pos = past + jnp.arange(L, dtype=jnp.int32)
        k_pos = jnp.arange(past + L, dtype=jnp.int32)
        o, _ = sb_piece(q, kf, vf, q_pos, k_pos, jnp.zeros((bsz, SB_HEADS, L), jnp.float32))
        o = jnp.transpose(o, (0, 2, 1, 3)).reshape(bsz, L, SB_HEADS * SB_HEAD_DIM)
    return o.astype(h.dtype) @ w_out, k, v


def softmax_with_sink(s, sink):
    m = jnp.maximum(jnp.max(s, axis=-1, keepdims=True), sink[..., None])
    p = jnp.exp(s - m)
    return p / (jnp.sum(p, axis=-1, keepdims=True) + jnp.exp(sink[..., None] - m))


def swa_mixer(h, past_k, past_v, w_qkv, sinks, w_out):
    bsz, L, _ = h.shape
    G, R, dh = SWA_KV_HEADS, SWA_Q_HEADS // SWA_KV_HEADS, SWA_HEAD_DIM
    proj = h @ w_qkv
    q = proj[..., :SWA_Q_HEADS * dh].reshape(bsz, L, G, R, dh)
    k = proj[..., SWA_Q_HEADS * dh:(SWA_Q_HEADS + G) * dh].reshape(bsz, L, G, dh)
    v = proj[..., (SWA_Q_HEADS + G) * dh:].reshape(bsz, L, G, dh)
    sink = sinks.astype(jnp.float32).reshape(G, R, 1)
    scale = dh ** -0.5
    if past_k is None:
        nC = L // CHUNK

        def band(t):
            tc = t.reshape(bsz, nC, CHUNK, G, dh)
            tp = jnp.pad(tc, ((0, 0), (SWA_BACK_CHUNKS, 0), (0, 0), (0, 0), (0, 0)))
            return jnp.concatenate([tp[:, j:j + nC] for j in range(SWA_BACK_CHUNKS + 1)], axis=2)

        kb, vb = band(k), band(v)
        qc = q.reshape(bsz, nC, CHUNK, G, R, dh)
        key_chunk = (jnp.arange(nC)[:, None] + (jnp.arange((SWA_BACK_CHUNKS + 1) * CHUNK) // CHUNK)[None, :]
                     - SWA_BACK_CHUNKS)
        valid = key_chunk >= 0
        s = jnp.einsum('bcqgrd,bckgd->bcgrqk', qc, kb).astype(jnp.float32) * scale
        s = jnp.where(valid[None, :, None, None, None, :], s, -jnp.inf)
        p = softmax_with_sink(s, sink)
        o = jnp.einsum('bcgrqk,bckgd->bcqgrd', p.astype(v.dtype), vb).reshape(bsz, L, SWA_Q_HEADS * dh)
        new_k, new_v = k[:, L - SWA_ROWS:], v[:, L - SWA_ROWS:]
    else:
        kf = jnp.concatenate([past_k.astype(k.dtype), k], axis=1)
        vf = jnp.concatenate([past_v.astype(v.dtype), v], axis=1)
        s = jnp.einsum('bqgrd,bkgd->bgrqk', q, kf).astype(jnp.float32) * scale
        p = softmax_with_sink(s, sink)
        o = jnp.einsum('bgrqk,bkgd->bqgrd', p.astype(v.dtype), vf).reshape(bsz, L, SWA_Q_HEADS * dh)
        n = kf.shape[1]
        new_k, new_v = kf[:, n - SWA_ROWS:], vf[:, n - SWA_ROWS:]
    return o @ w_out, new_k, new_v


def run_trunk(x, c, is_prompt, state_ssm, state_conv, cache_sb_k, cache_sb_v, cache_swa_k, cache_swa_v,
              ada_w, ada_b, norm_g, ffn_w_in, ffn_w_out,
              ssd_w_in, ssd_conv_w, ssd_conv_b, ssd_dt_bias, ssd_a_log, ssd_d, ssd_norm_g, ssd_w_out,
              sb_w_qkv, sb_w_out, swa_w_qkv, swa_sinks, swa_w_out):
    bsz, L, _ = x.shape
    c_act = jax.nn.silu(c)
    ssm_new, conv_new, sbk_new, sbv_new, swak_new, swav_new = [], [], [], [], [], []
    for i in range(DEPTH):
        mod = (c_act @ ada_w[i] + ada_b[i]).reshape(bsz, 3, 3, 1, D_MODEL)
        h = adaln_in(x, norm_g[i, 0], mod[:, 0, 0], mod[:, 0, 1])
        x = x + FFN_RESID * mod[:, 0, 2] * rmsnorm(swiglu(h, ffn_w_in[i, 0], ffn_w_out[i, 0]), norm_g[i, 1])
        h = adaln_in(x, norm_g[i, 2], mod[:, 1, 0], mod[:, 1, 1])
        kind, j = i % N_MIXERS, i // N_MIXERS
        if kind == 0:
            if is_prompt:
                conv0 = jnp.zeros((bsz, SSD_CONV - 1, SSD_CONV_DIM), x.dtype)
                ssm0 = jnp.zeros((bsz, SSD_HEADS, SSD_HEAD_DIM, SSD_STATE), x.dtype)
            else:
                conv0, ssm0 = state_conv[j], state_ssm[j]
            y, conv1, ssm1 = ssd_mixer(h, conv0, ssm0, ssd_w_in[j], ssd_conv_w[j], ssd_conv_b[j], ssd_dt_bias[j],
                                       ssd_a_log[j], ssd_d[j], ssd_norm_g[j], ssd_w_out[j],
                                       SSD_CHUNK if is_prompt else L)
            conv_new.append(conv1)
            ssm_new.append(ssm1)
        elif kind == 1:
            pk = None if is_prompt else cache_sb_k[j]
            pv = None if is_prompt else cache_sb_v[j]
            y, nk, nv = sb_mixer(h, pk, pv, sb_w_qkv[j], sb_w_out[j])
            sbk_new.append(nk)
            sbv_new.append(nv)
        else:
            pk = None if is_prompt else cache_swa_k[j]
            pv = None if is_prompt else cache_swa_v[j]
            y, nk, nv = swa_mixer(h, pk, pv, swa_w_qkv[j], swa_sinks[j], swa_w_out[j])
            swak_new.append(nk)
            swav_new.append(nv)
        x = x + mod[:, 1, 2] * rmsnorm(y, norm_g[i, 3])
        h = adaln_in(x, norm_g[i, 4], mod[:, 2, 0], mod[:, 2, 1])
        x = x + FFN_RESID * mod[:, 2, 2] * rmsnorm(swiglu(h, ffn_w_in[i, 1], ffn_w_out[i, 1]), norm_g[i, 5])
    return (x, jnp.stack(ssm_new), jnp.stack(conv_new), jnp.stack(sbk_new), jnp.stack(sbv_new),
            jnp.stack(swak_new), jnp.stack(swav_new))


def setup_inputs(seed: int = 0) -> dict:
    key = jax.random.key(seed)
    ks = jax.random.split(key, 32)
    f32 = jnp.float32

    def nrm(k, shape, scale):
        return jax.random.normal(k, shape, f32) * scale

    dt0 = jnp.exp(jax.random.uniform(ks[15], (N_SSD_LAYERS, SSD_HEADS), f32, math.log(1e-3), math.log(1e-1)))
    return {
        'x_prompt': nrm(ks[0], (BATCH, SEQ, D_MODEL), 1.0),
        'x_sample': nrm(ks[1], (DEC_BATCH, DEC_SEQ, D_MODEL), 1.0),
        'c_prompt': nrm(ks[2], (BATCH, D_MODEL), 1.0),
        'c_sample': nrm(ks[3], (DEC_BATCH, D_MODEL), 1.0),
        'state_ssm': nrm(ks[4], (N_SSD_LAYERS, DEC_BATCH, SSD_HEADS, SSD_HEAD_DIM, SSD_STATE), 0.5),
        'state_conv': nrm(ks[5], (N_SSD_LAYERS, DEC_BATCH, SSD_CONV - 1, SSD_CONV_DIM), 1.0),
        'cache_sb_k': nrm(ks[6], (N_SB_LAYERS, DEC_BATCH, PAST_LEN, SB_HEADS, SB_HEAD_DIM), 1.0),
        'cache_sb_v': nrm(ks[7], (N_SB_LAYERS, DEC_BATCH, PAST_LEN, SB_HEADS, SB_HEAD_DIM), 1.0),
        'cache_swa_k': nrm(ks[8], (N_SWA_LAYERS, DEC_BATCH, SWA_ROWS, SWA_KV_HEADS, SWA_HEAD_DIM), 1.0),
        'cache_swa_v': nrm(ks[9], (N_SWA_LAYERS, DEC_BATCH, SWA_ROWS, SWA_KV_HEADS, SWA_HEAD_DIM), 1.0),
        'ada_w': nrm(ks[10], (DEPTH, D_MODEL, 9 * D_MODEL), D_MODEL ** -0.5),
        'ada_b': nrm(ks[11], (DEPTH, 9 * D_MODEL), 0.01),
        'norm_g': 1.0 + nrm(ks[12], (DEPTH, 6, D_MODEL), 0.02),
        'ffn_w_in': nrm(ks[13], (DEPTH, 2, D_MODEL, 2 * D_FF), D_MODEL ** -0.5),
        'ffn_w_out': nrm(ks[14], (DEPTH, 2, D_FF, D_MODEL), D_FF ** -0.5),
        'ssd_w_in': nrm(ks[16], (N_SSD_LAYERS, D_MODEL, SSD_IN_DIM), D_MODEL ** -0.5),
        'ssd_conv_w': nrm(ks[17], (N_SSD_LAYERS, SSD_CONV, SSD_CONV_DIM), SSD_CONV ** -0.5),
        'ssd_conv_b': nrm(ks[18], (N_SSD_LAYERS, SSD_CONV_DIM), 0.02),
        'ssd_dt_bias': dt0 + jnp.log(-jnp.expm1(-dt0)),
        'ssd_a_log': jnp.log(jax.random.uniform(ks[19], (N_SSD_LAYERS, SSD_HEADS), f32, 1.0, 16.0)),
        'ssd_d': 1.0 + nrm(ks[20], (N_SSD_LAYERS, SSD_HEADS), 0.02),
        'ssd_norm_g': 1.0 + nrm(ks[21], (N_SSD_LAYERS, SSD_D_INNER), 0.02),
        'ssd_w_out': nrm(ks[22], (N_SSD_LAYERS, SSD_D_INNER, D_MODEL), SSD_D_INNER ** -0.5),
        'sb_w_qkv': nrm(ks[23], (N_SB_LAYERS, D_MODEL, 3 * SB_HEADS * SB_HEAD_DIM), D_MODEL ** -0.5),
        'sb_w_out': nrm(ks[24], (N_SB_LAYERS, SB_HEADS * SB_HEAD_DIM, D_MODEL), (SB_HEADS * SB_HEAD_DIM) ** -0.5),
        'swa_w_qkv': nrm(ks[25], (N_SWA_LAYERS, D_MODEL, (SWA_Q_HEADS + 2 * SWA_KV_HEADS) * SWA_HEAD_DIM),
                         D_MODEL ** -0.5),
        'swa_sinks': nrm(ks[26], (N_SWA_LAYERS, SWA_Q_HEADS), 0.5),
        'swa_w_out': nrm(ks[27], (N_SWA_LAYERS, SWA_Q_HEADS * SWA_HEAD_DIM, D_MODEL),
                         (SWA_Q_HEADS * SWA_HEAD_DIM) ** -0.5),
    }


def reference(x_prompt, x_sample, c_prompt, c_sample, state_ssm, state_conv, cache_sb_k, cache_sb_v,
              cache_swa_k, cache_swa_v, ada_w, ada_b, norm_g, ffn_w_in, ffn_w_out,
              ssd_w_in, ssd_conv_w, ssd_conv_b, ssd_dt_bias, ssd_a_log, ssd_d, ssd_norm_g, ssd_w_out,
              sb_w_qkv, sb_w_out, swa_w_qkv, swa_sinks, swa_w_out):
    y_prompt, ssm_p, conv_p, sbk_p, sbv_p, swak_p, swav_p = run_trunk(
        x_prompt, c_prompt, True, None, None, None, None, None, None,
        ada_w, ada_b, norm_g, ffn_w_in, ffn_w_out,
        ssd_w_in, ssd_conv_w, ssd_conv_b, ssd_dt_bias, ssd_a_log, ssd_d, ssd_norm_g, ssd_w_out,
        sb_w_qkv, sb_w_out, swa_w_qkv, swa_sinks, swa_w_out)
    y_sample, ssm_s, conv_s, sbk_s, sbv_s, swak_s, swav_s = run_trunk(
        x_sample, c_sample, False, state_ssm, state_conv, cache_sb_k, cache_sb_v, cache_swa_k, cache_swa_v,
        ada_w, ada_b, norm_g, ffn_w_in, ffn_w_out,
        ssd_w_in, ssd_conv_w, ssd_conv_b, ssd_dt_bias, ssd_a_log, ssd_d, ssd_norm_g, ssd_w_out,
        sb_w_qkv, sb_w_out, swa_w_qkv, swa_sinks, swa_w_out)
    return (y_prompt, y_sample, ssm_p, ssm_s, conv_p, conv_s, sbk_p, sbk_s, sbv_p, sbv_s,
            swak_p, swak_s, swav_p, swav_s)
```

```python
import functools

import jax
import jax.numpy as jnp
from jax import lax
from jax.experimental import pallas as pl
from jax.experimental.pallas import tpu as pltpu

F32 = jnp.float32
BF16 = jnp.bfloat16

D_MODEL = 1024
DEPTH = 4
N_MIXERS = 3
NORM_EPS = 1e-6
FFN_RESID = 0.5
D_FF = 2816

SSD_D_INNER = 2048
SSD_HEAD_DIM = 64
SSD_HEADS = 32
SSD_GROUPS = 4
SSD_STATE = 128
SSD_CONV = 4
SSD_GN = SSD_GROUPS * SSD_STATE
SSD_CONV_DIM = SSD_D_INNER + 2 * SSD_GN
SSD_PAIRS = SSD_HEADS // 2
SSD_CHUNK = 128

SB_HEADS = 16
SB_HEAD_DIM = 64
SB_BLOCK = 128

SWA_Q_HEADS = 16
SWA_KV_HEADS = 4
SWA_HEAD_DIM = 64
SWA_ROWS = 128
SWA_CHUNK = 64

LANES = 128
HALF = 64
VMEM_LIMIT = 48 * 1024 * 1024

FFN_CHUNK = D_FF // 2
FFN_SUB = (512, 512, 384)
PROJ_TN = 1024


def _params(sem):
    return pltpu.CompilerParams(dimension_semantics=sem, vmem_limit_bytes=VMEM_LIMIT)


def _rms(x, g):
    return x * lax.rsqrt(jnp.mean(x * x, axis=-1, keepdims=True) + NORM_EPS) * g


def _silu(x):
    return x * (1.0 / (1.0 + jnp.exp(-x)))


def _softplus(x):
    return jnp.maximum(x, 0.0) + jnp.log1p(jnp.exp(-jnp.abs(x)))


def _dot(a, b):
    return jnp.dot(a, b, preferred_element_type=F32)


def _dot_nt(a, b):
    return lax.dot_general(a, b, (((1,), (1,)), ((), ())), preferred_element_type=F32)


def _split3(a):
    hi = a.astype(BF16)
    r = a - hi.astype(F32)
    mid = r.astype(BF16)
    lo = (r - mid.astype(F32)).astype(BF16)
    return hi, mid, lo


def _mod_kernel(c_ref, w_ref, b_ref, o_ref):
    a = _silu(c_ref[...]).astype(BF16)
    o_ref[0] = _dot(a, w_ref[0].astype(BF16)) + b_ref[0]


def _modulation(c_all, ada_w, ada_b):
    rows = c_all.shape[0]
    tn = 1152
    n_out = ada_w.shape[-1]
    return pl.pallas_call(
        _mod_kernel,
        grid=(DEPTH, n_out // tn),
        in_specs=[
            pl.BlockSpec((rows, D_MODEL), lambda i, j: (0, 0)),
            pl.BlockSpec((1, D_MODEL, tn), lambda i, j: (i, 0, j)),
            pl.BlockSpec((1, 1, tn), lambda i, j: (i, 0, j)),
        ],
        out_specs=pl.BlockSpec((1, rows, tn), lambda i, j: (i, 0, j)),
        out_shape=jax.ShapeDtypeStruct((DEPTH, rows, n_out), F32),
        compiler_params=_params(("arbitrary", "arbitrary")),
        name="adaln_mod",
    )(c_all, ada_w, ada_b.reshape(DEPTH, 1, n_out))


def _modulated_norm(x_ref, mod_ref, g_ref, pre):
    m = mod_ref[0]
    return _rms(x_ref[...], g_ref[pre:pre + 1, :]) * (1.0 + m[1]) + m[0]


def _ffn_kernel(x_ref, mod_ref, g_ref, wa_ref, wb_ref, wo_ref, o_ref, h_sc, acc_sc, *, pre, post):
    j = pl.program_id(1)

    @pl.when(j == 0)
    def _():
        h_sc[...] = _modulated_norm(x_ref, mod_ref, g_ref, pre).astype(BF16)
        acc_sc[...] = jnp.zeros_like(acc_sc)

    h = h_sc[...]
    start = 0
    for width in FFN_SUB:
        a = _dot(h, wa_ref[:, start:start + width])
        b = _dot(h, wb_ref[:, start:start + width])
        acc_sc[...] += _dot((_silu(a) * b).astype(BF16), wo_ref[start:start + width, :])
        start += width

    @pl.when(j == pl.num_programs(1) - 1)
    def _():
        y = _rms(acc_sc[...], g_ref[post:post + 1, :])
        o_ref[...] = x_ref[...] + FFN_RESID * mod_ref[0][2] * y


def _ffn(x, mod, g6, w_in, w_out, *, pre, post, tm, tiles_per_mod):
    rows = x.shape[0]
    rm = mod.shape[2]
    n_chunks = D_FF // FFN_CHUNK
    return pl.pallas_call(
        functools.partial(_ffn_kernel, pre=pre, post=post),
        grid=(rows // tm, n_chunks),
        in_specs=[
            pl.BlockSpec((tm, D_MODEL), lambda i, j: (i, 0)),
            pl.BlockSpec((1, 3, rm, D_MODEL), lambda i, j: (i // tiles_per_mod, 0, 0, 0)),
            pl.BlockSpec((6, D_MODEL), lambda i, j: (0, 0)),
            pl.BlockSpec((D_MODEL, FFN_CHUNK), lambda i, j: (0, j)),
            pl.BlockSpec((D_MODEL, FFN_CHUNK), lambda i, j: (0, j + n_chunks)),
            pl.BlockSpec((FFN_CHUNK, D_MODEL), lambda i, j: (j, 0)),
        ],
        out_specs=pl.BlockSpec((tm, D_MODEL), lambda i, j: (i, 0)),
        out_shape=jax.ShapeDtypeStruct((rows, D_MODEL), F32),
        scratch_shapes=[pltpu.VMEM((tm, D_MODEL), BF16), pltpu.VMEM((tm, D_MODEL), F32)],
        compiler_params=_params(("arbitrary", "arbitrary")),
        name="ffn",
    )(x, mod, g6, w_in, w_in, w_out)


def _proj_kernel(x_ref, mod_ref, g_ref, w_ref, *rest, pre, has_extra):
    if has_extra:
        we_ref, o_ref, oe_ref, h_sc = rest
    else:
        o_ref, h_sc = rest
    j = pl.program_id(1)

    @pl.when(j == 0)
    def _():
        h_sc[...] = _modulated_norm(x_ref, mod_ref, g_ref, pre).astype(BF16)
        if has_extra:
            oe_ref[...] = _dot(h_sc[...], we_ref[...])

    o_ref[...] = _dot(h_sc[...], w_ref[...])


def _proj(x, mod, g6, w, w_extra, *, pre, tm, tn, tiles_per_mod):
    rows = x.shape[0]
    rm = mod.shape[2]
    n_out = w.shape[1]
    has_extra = w_extra is not None
    in_specs = [
        pl.BlockSpec((tm, D_MODEL), lambda i, j: (i, 0)),
        pl.BlockSpec((1, 3, rm, D_MODEL), lambda i, j: (i // tiles_per_mod, 0, 0, 0)),
        pl.BlockSpec((6, D_MODEL), lambda i, j: (0, 0)),
        pl.BlockSpec((D_MODEL, tn), lambda i, j: (0, j)),
    ]
    out_specs = [pl.BlockSpec((tm, tn), lambda i, j: (i, j))]
    out_shape = [jax.ShapeDtypeStruct((rows, n_out), F32)]
    args = [x, mod, g6, w]
    if has_extra:
        in_specs.append(pl.BlockSpec((D_MODEL, LANES), lambda i, j: (0, 0)))
        out_specs.append(pl.BlockSpec((tm, LANES), lambda i, j: (i, 0)))
        out_shape.append(jax.ShapeDtypeStruct((rows, LANES), F32))
        args.append(w_extra)
    return pl.pallas_call(
        functools.partial(_proj_kernel, pre=pre, has_extra=has_extra),
        grid=(rows // tm, n_out // tn),
        in_specs=in_specs,
        out_specs=out_specs,
        out_shape=out_shape,
        scratch_shapes=[pltpu.VMEM((tm, D_MODEL), BF16)],
        compiler_params=_params(("arbitrary", "arbitrary")),
        name="mixer_in_proj",
    )(*args)


def _outproj_kernel(y_ref, x_ref, mod_ref, g_ref, w_ref, o_ref, *, post):
    y = _dot(y_ref[...].astype(BF16), w_ref[...])
    o_ref[...] = x_ref[...] + mod_ref[0][2] * _rms(y, g_ref[post:post + 1, :])


def _outproj(y, x, mod, g6, w, *, post, tm, tiles_per_mod):
    rows = x.shape[0]
    rm = mod.shape[2]
    k_in = w.shape[0]
    return pl.pallas_call(
        functools.partial(_outproj_kernel, post=post),
        grid=(rows // tm,),
        in_specs=[
            pl.BlockSpec((tm, k_in), lambda i: (i, 0)),
            pl.BlockSpec((tm, D_MODEL), lambda i: (i, 0)),
            pl.BlockSpec((1, 3, rm, D_MODEL), lambda i: (i // tiles_per_mod, 0, 0, 0)),
            pl.BlockSpec((6, D_MODEL), lambda i: (0, 0)),
            pl.BlockSpec((k_in, D_MODEL), lambda i: (0, 0)),
        ],
        out_specs=pl.BlockSpec((tm, D_MODEL), lambda i: (i, 0)),
        out_shape=jax.ShapeDtypeStruct((rows, D_MODEL), F32),
        compiler_params=_params(("arbitrary",)),
        name="mixer_out_proj",
    )(y, x, mod, g6, w)


def _ssd_kernel(z_ref, x_ref, bc_ref, dt_ref, conv0_ref, ssm0_ref, cw_ref, cb_ref, dtb_ref, alog_ref, dsk_ref,
                ng_ref, tri_ref, exp_ref, y_ref, st_ref, convo_ref, xext_sc, act_sc, ax_sc, y_sc, *, n_valid):
    lc = SSD_CHUNK
    c = pl.program_id(1)

    @pl.when(c == 0)
    def _():
        xext_sc[0:8, :] = conv0_ref[0]
        st_ref[0] = ssm0_ref[0]

    xext_sc[8:8 + lc, 0:SSD_D_INNER] = x_ref[...]
    xext_sc[8:8 + lc, SSD_D_INNER:SSD_CONV_DIM] = bc_ref[...]
    cwid = 256
    for c0 in range(0, SSD_CONV_DIM, cwid):
        cols = slice(c0, c0 + cwid)
        conv = cb_ref[:, cols] + xext_sc[8:8 + lc, cols] * cw_ref[3:4, cols]
        for k in range(1, SSD_CONV):
            conv = conv + xext_sc[8 - k:8 - k + lc, cols] * cw_ref[3 - k:4 - k, cols]
        act_sc[:, cols] = _silu(conv)
    new_tail = xext_sc[n_valid:n_valid + 8, :]
    xext_sc[0:8, :] = new_tail
    convo_ref[0] = new_tail

    dt = _softplus(dt_ref[...] + dtb_ref[...])
    if n_valid < lc:
        dt = jnp.where(lax.broadcasted_iota(jnp.int32, (lc, LANES), 0) < n_valid, dt, 0.0)
    a = dt * (-jnp.exp(alog_ref[...]))
    tri = tri_ref[...]
    acum = sum(_dot(tri, part) for part in _split3(a))
    a_last = acum[lc - 1:lc, :]
    w_end = jnp.exp(a_last - acum) * dt
    acum_t = acum.T
    dt_t = dt.T
    w_t = w_end.T
    ax_sc[...] = sum(_dot(part, exp_ref[...]) for part in _split3(acum))

    row = lax.broadcasted_iota(jnp.int32, (lc, lc), 0)
    col = lax.broadcasted_iota(jnp.int32, (lc, lc), 1)
    causal = col <= row
    lane = lax.broadcasted_iota(jnp.int32, (lc, LANES), 1)
    lo_half = lane < HALF

    for g in range(SSD_GROUPS):
        b_g = act_sc[:, SSD_D_INNER + g * SSD_STATE:SSD_D_INNER + (g + 1) * SSD_STATE]
        c_g = act_sc[:, SSD_D_INNER + SSD_GN + g * SSD_STATE:SSD_D_INNER + SSD_GN + (g + 1) * SSD_STATE]
        cb = _dot_nt(c_g.astype(BF16), b_g.astype(BF16))
        b_gt = b_g.T
        for k in range(g * SSD_PAIRS // SSD_GROUPS, (g + 1) * SSD_PAIRS // SSD_GROUPS):
            xp = act_sc[:, k * LANES:(k + 1) * LANES]
            x_lo = jnp.where(lo_half, xp, 0.0)
            x_hi = jnp.where(lo_half, 0.0, xp)
            hp = st_ref[0, k]
            h_lo = jnp.where(lo_half, hp, 0.0)
            h_hi = jnp.where(lo_half, 0.0, hp)
            lhs, bw = [], []
            for h2 in range(2):
                h = 2 * k + h2
                ah = ax_sc[:, h * LANES:(h + 1) * LANES]
                seg = jnp.where(causal, ah - acum_t[h:h + 1, :], -1e30)
                lhs.append(cb * jnp.exp(seg) * dt_t[h:h + 1, :])
                bw.append(b_gt * w_t[h:h + 1, :])
            for h2 in range(2):
                h = 2 * k + h2
                lhs.append(c_g * jnp.exp(ax_sc[:, h * LANES:(h + 1) * LANES]))
            lhs = jnp.concatenate(lhs, axis=1).astype(BF16)
            rhs = jnp.concatenate([x_lo, x_hi, h_lo, h_hi], axis=0).astype(BF16)
            y_pair = _dot(lhs, rhs) + dsk_ref[:, k * LANES:(k + 1) * LANES] * xp
            y_sc[:, k * LANES:(k + 1) * LANES] = y_pair
            s_new = _dot(jnp.concatenate(bw, axis=1).astype(BF16),
                         jnp.concatenate([x_lo, x_hi], axis=0).astype(BF16))
            decay = jnp.where(lo_half[0:1, :],
                              jnp.exp(ax_sc[lc - 1:lc, (2 * k) * LANES:(2 * k + 1) * LANES]),
                              jnp.exp(ax_sc[lc - 1:lc, (2 * k + 1) * LANES:(2 * k + 2) * LANES]))
            st_ref[0, k] = hp * decay + s_new

    yz = y_sc[...] * _silu(z_ref[...])
    y_ref[...] = _rms(yz, ng_ref[...])


def _ssd_core(zxbc, dt_raw, conv0, ssm0, cw, cb, dtb, alog, dsk, ng, *, nseq, n_chunks, n_valid):
    lc = SSD_CHUNK
    rows = zxbc.shape[0]
    tri = (jnp.arange(lc)[:, None] >= jnp.arange(lc)[None, :]).astype(BF16)
    expand = (jnp.arange(SSD_HEADS * LANES)[None, :] // LANES == jnp.arange(LANES)[:, None]).astype(BF16)
    const = lambda shape: pl.BlockSpec(shape, lambda b, c: (0,) * len(shape))
    return pl.pallas_call(
        functools.partial(_ssd_kernel, n_valid=n_valid),
        grid=(nseq, n_chunks),
        in_specs=[
            pl.BlockSpec((lc, SSD_D_INNER), lambda b, c: (b * n_chunks + c, 0)),
            pl.BlockSpec((lc, SSD_D_INNER), lambda b, c: (b * n_chunks + c, 1)),
            pl.BlockSpec((lc, 2 * SSD_GN), lambda b, c: (b * n_chunks + c, 2 * SSD_D_INNER // (2 * SSD_GN))),
            pl.BlockSpec((lc, LANES), lambda b, c: (b * n_chunks + c, 0)),
            pl.BlockSpec((1, 8, SSD_CONV_DIM), lambda b, c: (b, 0, 0)),
            pl.BlockSpec((1, SSD_PAIRS, SSD_STATE, LANES), lambda b, c: (b, 0, 0, 0)),
            const((SSD_CONV, SSD_CONV_DIM)),
            const((1, SSD_CONV_DIM)),
            const((1, LANES)),
            const((1, LANES)),
            const((1, SSD_D_INNER)),
            const((1, SSD_D_INNER)),
            const((lc, lc)),
            const((LANES, SSD_HEADS * LANES)),
        ],
        out_specs=[
            pl.BlockSpec((lc, SSD_D_INNER), lambda b, c: (b * n_chunks + c, 0)),
            pl.BlockSpec((1, SSD_PAIRS, SSD_STATE, LANES), lambda b, c: (b, 0, 0, 0)),
            pl.BlockSpec((1, 8, SSD_CONV_DIM), lambda b, c: (b, 0, 0)),
        ],
        out_shape=[
            jax.ShapeDtypeStruct((rows, SSD_D_INNER), F32),
            jax.ShapeDtypeStruct((nseq, SSD_PAIRS, SSD_STATE, LANES), F32),
            jax.ShapeDtypeStruct((nseq, 8, SSD_CONV_DIM), F32),
        ],
        scratch_shapes=[
            pltpu.VMEM((lc + 8, SSD_CONV_DIM), F32),
            pltpu.VMEM((lc, SSD_CONV_DIM), F32),
            pltpu.VMEM((lc, SSD_HEADS * LANES), F32),
            pltpu.VMEM((lc, SSD_D_INNER), F32),
        ],
        compiler_params=_params(("arbitrary", "arbitrary")),
        name="ssd_core",
    )(zxbc, zxbc, zxbc, dt_raw, conv0, ssm0, cw, cb, dtb, alog, dsk, ng, tri, expand)


def _ssd_state_to_pairs(s):
    b = s.shape[0]
    s = s.reshape(b, SSD_PAIRS, 2, SSD_HEAD_DIM, SSD_STATE)
    return jnp.transpose(s, (0, 1, 4, 2, 3)).reshape(b, SSD_PAIRS, SSD_STATE, LANES)


def _ssd_state_from_pairs(s):
    b = s.shape[0]
    s = s.reshape(b, SSD_PAIRS, SSD_STATE, 2, SSD_HEAD_DIM)
    return jnp.transpose(s, (0, 1, 3, 4, 2)).reshape(b, SSD_HEADS, SSD_HEAD_DIM, SSD_STATE)


def _sb_kernel(q_ref, k_ref, v_ref, tt_ref, o_ref, acc_sc, out_sc, *, key_block_offset):
    blk = SB_BLOCK
    diag = pl.program_id(2) + key_block_offset
    row = lax.broadcasted_iota(jnp.int32, (blk, blk), 0)
    col = lax.broadcasted_iota(jnp.int32, (blk, blk), 1)
    lo_half = col < HALF
    before = col < row
    q = q_ref[...] * (SB_HEAD_DIM ** -0.5)
    q_heads = (jnp.where(lo_half, q, 0.0).astype(BF16), jnp.where(lo_half, 0.0, q).astype(BF16))
    acc_sc[...] = jnp.zeros_like(acc_sc)
    out_sc[...] = jnp.zeros_like(out_sc)

    def block(j, masked):
        start = pl.multiple_of(j * blk, blk)
        kb = k_ref[pl.ds(start, blk), :].astype(BF16)
        vb = v_ref[pl.ds(start, blk), :]
        v_stack = jnp.concatenate([jnp.where(lo_half, vb, 0.0), jnp.where(lo_half, 0.0, vb)], axis=0).astype(BF16)
        weights = []
        for h2 in range(2):
            z = _dot_nt(q_heads[h2], kb)
            u = jnp.minimum(-z, 0.0) - jnp.log1p(jnp.exp(-jnp.abs(z)))
            if masked:
                u = jnp.where(before, u, 0.0)
            u_hi = u.astype(BF16).astype(F32)
            rt = _dot(jnp.concatenate([u_hi, u - u_hi], axis=1).astype(BF16), tt_ref[...])
            acc = acc_sc[h2]
            w = jnp.exp(z + rt[:, :blk] + acc)
            if masked:
                w = jnp.where(before, w, 0.0)
            acc_sc[h2] = acc + rt[:, blk:]
            weights.append(w)
        out_sc[...] += _dot(jnp.concatenate(weights, axis=1).astype(BF16), v_stack)

    block(diag, True)

    def older(t, carry):
        block(diag - 1 - t, False)
        return carry

    lax.fori_loop(0, diag, older, 0)
    o_ref[...] = out_sc[...]


def _sb_attention(q_arr, q_col0, kv_arr, k_col0, v_col0, *, nseq, n_q_blocks, n_k_rows, key_block_offset):
    blk = SB_BLOCK
    n_pairs = SB_HEADS // 2
    ones = jnp.ones((blk, blk), F32)
    tri = (jnp.arange(blk)[:, None] >= jnp.arange(blk)[None, :]).astype(F32)
    half = jnp.concatenate([tri, ones], axis=1)
    tt = jnp.concatenate([half, half], axis=0).astype(BF16)
    return pl.pallas_call(
        functools.partial(_sb_kernel, key_block_offset=key_block_offset),
        grid=(nseq, n_pairs, n_q_blocks),
        in_specs=[
            pl.BlockSpec((blk, LANES), lambda b, p, i: (b * n_q_blocks + i, q_col0 + p)),
            pl.BlockSpec((n_k_rows, LANES), lambda b, p, i: (b, k_col0 + p)),
            pl.BlockSpec((n_k_rows, LANES), lambda b, p, i: (b, v_col0 + p)),
            pl.BlockSpec((2 * blk, 2 * blk), lambda b, p, i: (0, 0)),
        ],
        out_specs=pl.BlockSpec((blk, LANES), lambda b, p, i: (b * n_q_blocks + i, p)),
        out_shape=jax.ShapeDtypeStruct((nseq * n_q_blocks * blk, SB_HEADS * SB_HEAD_DIM), F32),
        scratch_shapes=[pltpu.VMEM((2, blk, blk), F32), pltpu.VMEM((blk, LANES), F32)],
        compiler_params=_params(("arbitrary", "arbitrary", "arbitrary")),
        name="sb_attention",
    )(q_arr, kv_arr, kv_arr, tt)


def _swa_heads(q, kk, vv, sink_ref, valid):
    tq = q.shape[0]
    lane = lax.broadcasted_iota(jnp.int32, (tq, LANES), 1)
    lo_half = lane < HALF
    rep = SWA_Q_HEADS // SWA_KV_HEADS

    def aligned(x, g):
        blk = x[:, (g // 2) * LANES:(g // 2 + 1) * LANES]
        swapped = pltpu.roll(blk, HALF, axis=1)
        return (blk, swapped) if g % 2 == 0 else (swapped, blk)

    outs = []
    for hp in range(SWA_Q_HEADS // 2):
        qb = q[:, hp * LANES:(hp + 1) * LANES] * (SWA_HEAD_DIM ** -0.5)
        res = []
        for h2 in range(2):
            h = 2 * hp + h2
            g = h // rep
            qm = (jnp.where(lo_half, qb, 0.0) if h2 == 0 else jnp.where(lo_half, 0.0, qb)).astype(BF16)
            s = _dot_nt(qm, aligned(kk, g)[h2].astype(BF16))
            if valid is not None:
                s = jnp.where(valid, s, -jnp.inf)
            sink = sink_ref[h]
            m = jnp.maximum(jnp.max(s, axis=-1, keepdims=True), sink)
            p = jnp.exp(s - m)
            p = p * (1.0 / (jnp.sum(p, axis=-1, keepdims=True) + jnp.exp(sink - m)))
            res.append(_dot(p.astype(BF16), aligned(vv, g)[h2].astype(BF16)))
        outs.append(jnp.where(lo_half, res[0], res[1]))
    return jnp.concatenate(outs, axis=1)


def _swa_prompt_kernel(sink_ref, q_ref, kp_ref, kc_ref, vp_ref, vc_ref, o_ref):
    i = pl.program_id(1)
    blk = 2 * SWA_CHUNK
    kk = jnp.concatenate([kp_ref[...], kc_ref[...]], axis=0)
    vv = jnp.concatenate([vp_ref[...], vc_ref[...]], axis=0)
    q_chunk = lax.broadcasted_iota(jnp.int32, (blk, 2 * blk), 0) // SWA_CHUNK
    k_chunk = lax.broadcasted_iota(jnp.int32, (blk, 2 * blk), 1) // SWA_CHUNK
    valid = (k_chunk >= q_chunk) & (k_chunk <= q_chunk + 2) & ((i > 0) | (k_chunk >= 2))
    o_ref[...] = _swa_heads(q_ref[...], kk, vv, sink_ref, valid)


def _swa_prompt(qkv, sinks, *, nseq, seq_len):
    blk = 2 * SWA_CHUNK
    nb = seq_len // blk
    dq = SWA_Q_HEADS * SWA_HEAD_DIM
    dkv = SWA_KV_HEADS * SWA_HEAD_DIM
    k_col, v_col = dq // dkv, dq // dkv + 1
    prev = lambda col: (lambda b, i: (b * nb + jnp.maximum(i - 1, 0), col))
    cur = lambda col: (lambda b, i: (b * nb + i, col))
    return pl.pallas_call(
        _swa_prompt_kernel,
        grid=(nseq, nb),
        in_specs=[
            pl.BlockSpec(memory_space=pltpu.SMEM),
            pl.BlockSpec((blk, dq), cur(0)),
            pl.BlockSpec((blk, dkv), prev(k_col)),
            pl.BlockSpec((blk, dkv), cur(k_col)),
            pl.BlockSpec((blk, dkv), prev(v_col)),
            pl.BlockSpec((blk, dkv), cur(v_col)),
        ],
        out_specs=pl.BlockSpec((blk, dq), cur(0)),
        out_shape=jax.ShapeDtypeStruct((nseq * seq_len, dq), F32),
        compiler_params=_params(("arbitrary", "arbitrary")),
        name="swa_prompt",
    )(sinks, qkv, qkv, qkv, qkv, qkv)


def _swa_step_kernel(sink_ref, q_ref, k_ref, v_ref, o_ref):
    o_ref[...] = _swa_heads(q_ref[...], k_ref[0], v_ref[0], sink_ref, None)


def _swa_step(qkv, kf, vf, sinks, *, nseq, seq_len):
    dq = SWA_Q_HEADS * SWA_HEAD_DIM
    tk = kf.shape[1]
    return pl.pallas_call(
        _swa_step_kernel,
        grid=(nseq,),
        in_specs=[
            pl.BlockSpec(memory_space=pltpu.SMEM),
            pl.BlockSpec((seq_len, dq), lambda b: (b, 0)),
            pl.BlockSpec((1, tk, kf.shape[2]), lambda b: (b, 0, 0)),
            pl.BlockSpec((1, tk, vf.shape[2]), lambda b: (b, 0, 0)),
        ],
        out_specs=pl.BlockSpec((seq_len, dq), lambda b: (b, 0)),
        out_shape=jax.ShapeDtypeStruct((nseq * seq_len, dq), F32),
        compiler_params=_params(("arbitrary",)),
        name="swa_step",
    )(sinks, qkv, kf, vf)


def _pad_lanes(v, width=LANES):
    return jnp.pad(v, ((0, 0), (0, width - v.shape[-1])))


def _trunk(x, mods, is_prompt, state, weights):
    (state_ssm, state_conv, cache_sb_k, cache_sb_v, cache_swa_k, cache_swa_v) = state
    nseq, seq_len, _ = x.shape
    rows = nseq * seq_len
    x = x.reshape(rows, D_MODEL)
    if is_prompt:
        tm, tiles_per_mod = 512, seq_len // 512
    else:
        tm, tiles_per_mod = rows, 1
    row_kw = dict(tm=tm, tiles_per_mod=tiles_per_mod)
    ssm_new, conv_new, sbk_new, sbv_new, swak_new, swav_new = [], [], [], [], [], []
    for i in range(DEPTH):
        g6 = weights["norm_g"][i]
        x = _ffn(x, mods[i][0], g6, weights["ffn_w_in"][i, 0], weights["ffn_w_out"][i, 0], pre=0, post=1, **row_kw)
        kind, j = i % N_MIXERS, i // N_MIXERS
        mod = mods[i][1]
        if kind == 0:
            w_in = weights["ssd_w_in"][j]
            n_main = SSD_D_INNER + SSD_CONV_DIM
            zxbc, dt_raw = _proj(x, mod, g6, w_in[:, :n_main], _pad_lanes(w_in[:, n_main:]), pre=2, tn=PROJ_TN,
                                 **row_kw)
            lc = SSD_CHUNK
            if is_prompt:
                n_chunks, n_valid = seq_len // lc, lc
                conv0 = jnp.zeros((nseq, 8, SSD_CONV_DIM), F32)
                ssm0 = jnp.zeros((nseq, SSD_PAIRS, SSD_STATE, LANES), F32)
            else:
                n_chunks, n_valid = 1, seq_len
                pad = lambda t: jnp.pad(t.reshape(nseq, seq_len, -1), ((0, 0), (0, lc - seq_len), (0, 0))
                                        ).reshape(nseq * lc, -1)
                zxbc, dt_raw = pad(zxbc), pad(dt_raw)
                conv0 = jnp.pad(state_conv[j], ((0, 0), (8 - (SSD_CONV - 1), 0), (0, 0)))
                ssm0 = _ssd_state_to_pairs(state_ssm[j])
            y, st, conv_tail = _ssd_core(
                zxbc, dt_raw, conv0, ssm0, weights["ssd_conv_w"][j], weights["ssd_conv_b"][j][None],
                _pad_lanes(weights["ssd_dt_bias"][j][None]), _pad_lanes(weights["ssd_a_log"][j][None]),
                jnp.repeat(weights["ssd_d"][j], SSD_HEAD_DIM)[None], weights["ssd_norm_g"][j][None],
                nseq=nseq, n_chunks=n_chunks, n_valid=n_valid)
            if not is_prompt:
                y = y.reshape(nseq, lc, SSD_D_INNER)[:, :seq_len].reshape(rows, SSD_D_INNER)
            ssm_new.append(_ssd_state_from_pairs(st))
            conv_new.append(conv_tail[:, 8 - (SSD_CONV - 1):])
            w_out = weights["ssd_w_out"][j]
        elif kind == 1:
            qkv = _proj(x, mod, g6, weights["sb_w_qkv"][j], None, pre=2, tn=PROJ_TN, **row_kw)[0]
            dh = SB_HEADS * SB_HEAD_DIM
            k_new, v_new = qkv[:, dh:2 * dh], qkv[:, 2 * dh:]
            n_pairs = SB_HEADS // 2
            if is_prompt:
                y = _sb_attention(qkv, 0, qkv, n_pairs, 2 * n_pairs, nseq=nseq, n_q_blocks=seq_len // SB_BLOCK,
                                  n_k_rows=seq_len, key_block_offset=0)
            else:
                past = cache_sb_k.shape[2]
                pad_rows = lambda t: jnp.pad(t.reshape(nseq, seq_len, dh), ((0, 0), (0, SB_BLOCK - seq_len), (0, 0)))
                kf = jnp.concatenate([cache_sb_k[j].reshape(nseq, past, dh), pad_rows(k_new)], axis=1)
                vf = jnp.concatenate([cache_sb_v[j].reshape(nseq, past, dh), pad_rows(v_new)], axis=1)
                kv = jnp.concatenate([kf, vf], axis=2).reshape(nseq * (past + SB_BLOCK), 2 * dh)
                qp = pad_rows(qkv[:, :dh]).reshape(nseq * SB_BLOCK, dh)
                y = _sb_attention(qp, 0, kv, 0, n_pairs, nseq=nseq, n_q_blocks=1, n_k_rows=past + SB_BLOCK,
                                  key_block_offset=past // SB_BLOCK)
                y = y.reshape(nseq, SB_BLOCK, dh)[:, :seq_len].reshape(rows, dh)
            sbk_new.append(k_new.reshape(nseq, seq_len, SB_HEADS, SB_HEAD_DIM))
            sbv_new.append(v_new.reshape(nseq, seq_len, SB_HEADS, SB_HEAD_DIM))
            w_out = weights["sb_w_out"][j]
        else:
            dq, dkv = SWA_Q_HEADS * SWA_HEAD_DIM, SWA_KV_HEADS * SWA_HEAD_DIM
            qkv = _proj(x, mod, g6, weights["swa_w_qkv"][j], None, pre=2, tn=dq + 2 * dkv, **row_kw)[0]
            k_new = qkv[:, dq:dq + dkv].reshape(nseq, seq_len, dkv)
            v_new = qkv[:, dq + dkv:].reshape(nseq, seq_len, dkv)
            sinks = weights["swa_sinks"][j]
            if is_prompt:
                y = _swa_prompt(qkv, sinks, nseq=nseq, seq_len=seq_len)
                kf, vf = k_new, v_new
            else:
                kf = jnp.concatenate([cache_swa_k[j].reshape(nseq, -1, dkv), k_new], axis=1)
                vf = jnp.concatenate([cache_swa_v[j].reshape(nseq, -1, dkv), v_new], axis=1)
                y = _swa_step(qkv, kf, vf, sinks, nseq=nseq, seq_len=seq_len)
            n = kf.shape[1]
            swak_new.append(kf[:, n - SWA_ROWS:].reshape(nseq, SWA_ROWS, SWA_KV_HEADS, SWA_HEAD_DIM))
            swav_new.append(vf[:, n - SWA_ROWS:].reshape(nseq, SWA_ROWS, SWA_KV_HEADS, SWA_HEAD_DIM))
            w_out = weights["swa_w_out"][j]
        x = _outproj(y, x, mod, g6, w_out, post=3, **row_kw)
        x = _ffn(x, mods[i][2], g6, weights["ffn_w_in"][i, 1], weights["ffn_w_out"][i, 1], pre=4, post=5, **row_kw)
    return (x.reshape(nseq, seq_len, D_MODEL), jnp.stack(ssm_new), jnp.stack(conv_new), jnp.stack(sbk_new),
            jnp.stack(sbv_new), jnp.stack(swak_new), jnp.stack(swav_new))


def kernel(x_prompt, x_sample, c_prompt, c_sample, state_ssm, state_conv, cache_sb_k, cache_sb_v, cache_swa_k,
           cache_swa_v, ada_w, ada_b, norm_g, ffn_w_in, ffn_w_out, ssd_w_in, ssd_conv_w, ssd_conv_b, ssd_dt_bias,
           ssd_a_log, ssd_d, ssd_norm_g, ssd_w_out, sb_w_qkv, sb_w_out, swa_w_qkv, swa_sinks, swa_w_out):
    n_p, n_s = c_prompt.shape[0], c_sample.shape[0]
    s_len = x_sample.shape[1]
    c_all = jnp.concatenate([c_prompt, c_sample, jnp.zeros((16 - n_p - n_s, D_MODEL), F32)], axis=0)
    mod_all = _modulation(c_all, ada_w, ada_b).reshape(DEPTH, 16, 3, 3, 1, D_MODEL)
    mods_p = [[mod_all[i, :n_p, s] for s in range(3)] for i in range(DEPTH)]
    mods_s = [[jnp.transpose(jnp.repeat(mod_all[i, n_p:n_p + n_s, s, :, 0], s_len, axis=0), (1, 0, 2))[None]
               for s in range(3)] for i in range(DEPTH)]
    weights = dict(
        norm_g=norm_g, ffn_w_in=ffn_w_in.astype(BF16), ffn_w_out=ffn_w_out.astype(BF16),
        ssd_w_in=ssd_w_in.astype(BF16), ssd_conv_w=ssd_conv_w, ssd_conv_b=ssd_conv_b, ssd_dt_bias=ssd_dt_bias,
        ssd_a_log=ssd_a_log, ssd_d=ssd_d, ssd_norm_g=ssd_norm_g, ssd_w_out=ssd_w_out.astype(BF16),
        sb_w_qkv=sb_w_qkv.astype(BF16), sb_w_out=sb_w_out.astype(BF16), swa_w_qkv=swa_w_qkv.astype(BF16),
        swa_sinks=swa_sinks, swa_w_out=swa_w_out.astype(BF16))
    y_p, ssm_p, conv_p, sbk_p, sbv_p, swak_p, swav_p = _trunk(x_prompt, mods_p, True, (None,) * 6, weights)
    y_s, ssm_s, conv_s, sbk_s, sbv_s, swak_s, swav_s = _trunk(
        x_sample, mods_s, False, (state_ssm, state_conv, cache_sb_k, cache_sb_v, cache_swa_k, cache_swa_v), weights)
    return (y_p, y_s, ssm_p, ssm_s, conv_p, conv_s, sbk_p, sbk_s, sbv_p, sbv_s, swak_p, swak_s, swav_p, swav_s)
```

```python
import functools

import jax
import jax.numpy as jnp
from jax import lax
from jax.experimental import pallas as pl
from jax.experimental.pallas import tpu as pltpu

F32 = jnp.float32
BF16 = jnp.bfloat16

D_MODEL = 1024
DEPTH = 4
N_MIXERS = 3
NORM_EPS = 1e-6
FFN_RESID = 0.5
D_FF = 2816

SSD_D_INNER = 2048
SSD_HEAD_DIM = 64
SSD_HEADS = 32
SSD_GROUPS = 4
SSD_STATE = 128
SSD_CONV = 4
SSD_GN = SSD_GROUPS * SSD_STATE
SSD_CONV_DIM = SSD_D_INNER + 2 * SSD_GN
SSD_PAIRS = SSD_HEADS // 2
SSD_CHUNK = 128

SB_HEADS = 16
SB_HEAD_DIM = 64
SB_BLOCK = 128
SB_GROUP = 4
LOG2E = 1.4426950408889634

SWA_Q_HEADS = 16
SWA_KV_HEADS = 4
SWA_HEAD_DIM = 64
SWA_ROWS = 128
SWA_CHUNK = 64

LANES = 128
HALF = 64
VMEM_LIMIT = 48 * 1024 * 1024

FFN_CHUNK = D_FF // 2
FFN_SUB = (512, 512, 384)
PROJ_TN = 1024


def _params(sem):
    return pltpu.CompilerParams(dimension_semantics=sem, vmem_limit_bytes=VMEM_LIMIT)


def _rms(x, g):
    return x * lax.rsqrt(jnp.mean(x * x, axis=-1, keepdims=True) + NORM_EPS) * g


def _silu(x):
    return x * (1.0 / (1.0 + jnp.exp(-x)))


def _softplus(x):
    return jnp.maximum(x, 0.0) + jnp.log1p(jnp.exp(-jnp.abs(x)))


def _dot(a, b):
    return jnp.dot(a, b, preferred_element_type=F32)


def _dot_nt(a, b):
    return lax.dot_general(a, b, (((1,), (1,)), ((), ())), preferred_element_type=F32)


def _split3(a):
    hi = a.astype(BF16)
    r = a - hi.astype(F32)
    mid = r.astype(BF16)
    lo = (r - mid.astype(F32)).astype(BF16)
    return hi, mid, lo


def _mod_kernel(c_ref, w_ref, b_ref, o_ref):
    a = _silu(c_ref[...]).astype(BF16)
    o_ref[0] = _dot(a, w_ref[0].astype(BF16)) + b_ref[0]


def _modulation(c_all, ada_w, ada_b):
    rows = c_all.shape[0]
    tn = 1152
    n_out = ada_w.shape[-1]
    return pl.pallas_call(
        _mod_kernel,
        grid=(DEPTH, n_out // tn),
        in_specs=[
            pl.BlockSpec((rows, D_MODEL), lambda i, j: (0, 0)),
            pl.BlockSpec((1, D_MODEL, tn), lambda i, j: (i, 0, j)),
            pl.BlockSpec((1, 1, tn), lambda i, j: (i, 0, j)),
        ],
        out_specs=pl.BlockSpec((1, rows, tn), lambda i, j: (i, 0, j)),
        out_shape=jax.ShapeDtypeStruct((DEPTH, rows, n_out), F32),
        compiler_params=_params(("arbitrary", "arbitrary")),
        name="adaln_mod",
    )(c_all, ada_w, ada_b.reshape(DEPTH, 1, n_out))


def _modulated_norm(x_ref, mod_ref, g_ref, pre):
    m = mod_ref[0]
    return _rms(x_ref[...], g_ref[pre:pre + 1, :]) * (1.0 + m[1]) + m[0]


def _ffn_kernel(x_ref, mod_ref, g_ref, wa_ref, wb_ref, wo_ref, o_ref, h_sc, acc_sc, *, pre, post):
    j = pl.program_id(1)

    @pl.when(j == 0)
    def _():
        h_sc[...] = _modulated_norm(x_ref, mod_ref, g_ref, pre).astype(BF16)
        acc_sc[...] = jnp.zeros_like(acc_sc)

    h = h_sc[...]
    start = 0
    for width in FFN_SUB:
        a = _dot(h, wa_ref[:, start:start + width])
        b = _dot(h, wb_ref[:, start:start + width])
        acc_sc[...] += _dot((_silu(a) * b).astype(BF16), wo_ref[start:start + width, :])
        start += width

    @pl.when(j == pl.num_programs(1) - 1)
    def _():
        y = _rms(acc_sc[...], g_ref[post:post + 1, :])
        o_ref[...] = x_ref[...] + FFN_RESID * mod_ref[0][2] * y


def _ffn(x, mod, g6, w_in, w_out, *, pre, post, tm, tiles_per_mod):
    rows = x.shape[0]
    rm = mod.shape[2]
    n_chunks = D_FF // FFN_CHUNK
    return pl.pallas_call(
        functools.partial(_ffn_kernel, pre=pre, post=post),
        grid=(rows // tm, n_chunks),
        in_specs=[
            pl.BlockSpec((tm, D_MODEL), lambda i, j: (i, 0)),
            pl.BlockSpec((1, 3, rm, D_MODEL), lambda i, j: (i // tiles_per_mod, 0, 0, 0)),
            pl.BlockSpec((6, D_MODEL), lambda i, j: (0, 0)),
            pl.BlockSpec((D_MODEL, FFN_CHUNK), lambda i, j: (0, j)),
            pl.BlockSpec((D_MODEL, FFN_CHUNK), lambda i, j: (0, j + n_chunks)),
            pl.BlockSpec((FFN_CHUNK, D_MODEL), lambda i, j: (j, 0)),
        ],
        out_specs=pl.BlockSpec((tm, D_MODEL), lambda i, j: (i, 0)),
        out_shape=jax.ShapeDtypeStruct((rows, D_MODEL), F32),
        scratch_shapes=[pltpu.VMEM((tm, D_MODEL), BF16), pltpu.VMEM((tm, D_MODEL), F32)],
        compiler_params=_params(("arbitrary", "arbitrary")),
        name="ffn",
    )(x, mod, g6, w_in, w_in, w_out)


def _proj_kernel(x_ref, mod_ref, g_ref, w_ref, *rest, pre, has_extra):
    if has_extra:
        we_ref, o_ref, oe_ref, h_sc = rest
    else:
        o_ref, h_sc = rest
    j = pl.program_id(1)

    @pl.when(j == 0)
    def _():
        h_sc[...] = _modulated_norm(x_ref, mod_ref, g_ref, pre).astype(BF16)
        if has_extra:
            oe_ref[...] = _dot(h_sc[...], we_ref[...])

    o_ref[...] = _dot(h_sc[...], w_ref[...])


def _proj(x, mod, g6, w, w_extra, *, pre, tm, tn, tiles_per_mod):
    rows = x.shape[0]
    rm = mod.shape[2]
    n_out = w.shape[1]
    has_extra = w_extra is not None
    in_specs = [
        pl.BlockSpec((tm, D_MODEL), lambda i, j: (i, 0)),
        pl.BlockSpec((1, 3, rm, D_MODEL), lambda i, j: (i // tiles_per_mod, 0, 0, 0)),
        pl.BlockSpec((6, D_MODEL), lambda i, j: (0, 0)),
        pl.BlockSpec((D_MODEL, tn), lambda i, j: (0, j)),
    ]
    out_specs = [pl.BlockSpec((tm, tn), lambda i, j: (i, j))]
    out_shape = [jax.ShapeDtypeStruct((rows, n_out), F32)]
    args = [x, mod, g6, w]
    if has_extra:
        in_specs.append(pl.BlockSpec((D_MODEL, LANES), lambda i, j: (0, 0)))
        out_specs.append(pl.BlockSpec((tm, LANES), lambda i, j: (i, 0)))
        out_shape.append(jax.ShapeDtypeStruct((rows, LANES), F32))
        args.append(w_extra)
    return pl.pallas_call(
        functools.partial(_proj_kernel, pre=pre, has_extra=has_extra),
        grid=(rows // tm, n_out // tn),
        in_specs=in_specs,
        out_specs=out_specs,
        out_shape=out_shape,
        scratch_shapes=[pltpu.VMEM((tm, D_MODEL), BF16)],
        compiler_params=_params(("arbitrary", "arbitrary")),
        name="mixer_in_proj",
    )(*args)


def _outproj_kernel(y_ref, x_ref, mod_ref, g_ref, w_ref, o_ref, *, post):
    y = _dot(y_ref[...].astype(BF16), w_ref[...])
    o_ref[...] = x_ref[...] + mod_ref[0][2] * _rms(y, g_ref[post:post + 1, :])


def _outproj(y, x, mod, g6, w, *, post, tm, tiles_per_mod):
    rows = x.shape[0]
    rm = mod.shape[2]
    k_in = w.shape[0]
    return pl.pallas_call(
        functools.partial(_outproj_kernel, post=post),
        grid=(rows // tm,),
        in_specs=[
            pl.BlockSpec((tm, k_in), lambda i: (i, 0)),
            pl.BlockSpec((tm, D_MODEL), lambda i: (i, 0)),
            pl.BlockSpec((1, 3, rm, D_MODEL), lambda i: (i // tiles_per_mod, 0, 0, 0)),
            pl.BlockSpec((6, D_MODEL), lambda i: (0, 0)),
            pl.BlockSpec((k_in, D_MODEL), lambda i: (0, 0)),
        ],
        out_specs=pl.BlockSpec((tm, D_MODEL), lambda i: (i, 0)),
        out_shape=jax.ShapeDtypeStruct((rows, D_MODEL), F32),
        compiler_params=_params(("arbitrary",)),
        name="mixer_out_proj",
    )(y, x, mod, g6, w)


def _ssd_kernel(z_ref, x_ref, bc_ref, dt_ref, conv0_ref, ssm0_ref, cw_ref, cb_ref, dtb_ref, alog_ref, dsk_ref,
                ng_ref, tri_ref, exp_ref, y_ref, st_ref, convo_ref, xext_sc, act_sc, ax_sc, y_sc, *, n_valid):
    lc = SSD_CHUNK
    c = pl.program_id(1)

    @pl.when(c == 0)
    def _():
        xext_sc[0:8, :] = conv0_ref[0]
        st_ref[0] = ssm0_ref[0]

    xext_sc[8:8 + lc, 0:SSD_D_INNER] = x_ref[...]
    xext_sc[8:8 + lc, SSD_D_INNER:SSD_CONV_DIM] = bc_ref[...]
    cwid = 256
    for c0 in range(0, SSD_CONV_DIM, cwid):
        cols = slice(c0, c0 + cwid)
        conv = cb_ref[:, cols] + xext_sc[8:8 + lc, cols] * cw_ref[3:4, cols]
        for k in range(1, SSD_CONV):
            conv = conv + xext_sc[8 - k:8 - k + lc, cols] * cw_ref[3 - k:4 - k, cols]
        act_sc[:, cols] = _silu(conv)
    new_tail = xext_sc[n_valid:n_valid + 8, :]
    xext_sc[0:8, :] = new_tail
    convo_ref[0] = new_tail

    dt = _softplus(dt_ref[...] + dtb_ref[...])
    if n_valid < lc:
        dt = jnp.where(lax.broadcasted_iota(jnp.int32, (lc, LANES), 0) < n_valid, dt, 0.0)
    a = dt * (-jnp.exp(alog_ref[...]))
    tri = tri_ref[...]
    acum = sum(_dot(tri, part) for part in _split3(a))
    a_last = acum[lc - 1:lc, :]
    w_end = jnp.exp(a_last - acum) * dt
    acum_t = acum.T
    dt_t = dt.T
    w_t = w_end.T
    ax_sc[...] = sum(_dot(part, exp_ref[...]) for part in _split3(acum))

    row = lax.broadcasted_iota(jnp.int32, (lc, lc), 0)
    col = lax.broadcasted_iota(jnp.int32, (lc, lc), 1)
    causal = col <= row
    lane = lax.broadcasted_iota(jnp.int32, (lc, LANES), 1)
    lo_half = lane < HALF

    for g in range(SSD_GROUPS):
        b_g = act_sc[:, SSD_D_INNER + g * SSD_STATE:SSD_D_INNER + (g + 1) * SSD_STATE]
        c_g = act_sc[:, SSD_D_INNER + SSD_GN + g * SSD_STATE:SSD_D_INNER + SSD_GN + (g + 1) * SSD_STATE]
        cb = _dot_nt(c_g.astype(BF16), b_g.astype(BF16))
        b_gt = b_g.T
        for k in range(g * SSD_PAIRS // SSD_GROUPS, (g + 1) * SSD_PAIRS // SSD_GROUPS):
            xp = act_sc[:, k * LANES:(k + 1) * LANES]
            x_lo = jnp.where(lo_half, xp, 0.0)
            x_hi = jnp.where(lo_half, 0.0, xp)
            hp = st_ref[0, k]
            h_lo = jnp.where(lo_half, hp, 0.0)
            h_hi = jnp.where(lo_half, 0.0, hp)
            lhs, bw = [], []
            for h2 in range(2):
                h = 2 * k + h2
                ah = ax_sc[:, h * LANES:(h + 1) * LANES]
                seg = jnp.where(causal, ah - acum_t[h:h + 1, :], -1e30)
                lhs.append(cb * jnp.exp(seg) * dt_t[h:h + 1, :])
                bw.append(b_gt * w_t[h:h + 1, :])
            for h2 in range(2):
                h = 2 * k + h2
                lhs.append(c_g * jnp.exp(ax_sc[:, h * LANES:(h + 1) * LANES]))
            lhs = jnp.concatenate(lhs, axis=1).astype(BF16)
            rhs = jnp.concatenate([x_lo, x_hi, h_lo, h_hi], axis=0).astype(BF16)
            y_pair = _dot(lhs, rhs) + dsk_ref[:, k * LANES:(k + 1) * LANES] * xp
            y_sc[:, k * LANES:(k + 1) * LANES] = y_pair
            s_new = _dot(jnp.concatenate(bw, axis=1).astype(BF16),
                         jnp.concatenate([x_lo, x_hi], axis=0).astype(BF16))
            decay = jnp.where(lo_half[0:1, :],
                              jnp.exp(ax_sc[lc - 1:lc, (2 * k) * LANES:(2 * k + 1) * LANES]),
                              jnp.exp(ax_sc[lc - 1:lc, (2 * k + 1) * LANES:(2 * k + 2) * LANES]))
            st_ref[0, k] = hp * decay + s_new

    yz = y_sc[...] * _silu(z_ref[...])
    y_ref[...] = _rms(yz, ng_ref[...])


def _ssd_core(zxbc, dt_raw, conv0, ssm0, cw, cb, dtb, alog, dsk, ng, *, nseq, n_chunks, n_valid):
    lc = SSD_CHUNK
    rows = zxbc.shape[0]
    tri = (jnp.arange(lc)[:, None] >= jnp.arange(lc)[None, :]).astype(BF16)
    expand = (jnp.arange(SSD_HEADS * LANES)[None, :] // LANES == jnp.arange(LANES)[:, None]).astype(BF16)
    const = lambda shape: pl.BlockSpec(shape, lambda b, c: (0,) * len(shape))
    return pl.pallas_call(
        functools.partial(_ssd_kernel, n_valid=n_valid),
        grid=(nseq, n_chunks),
        in_specs=[
            pl.BlockSpec((lc, SSD_D_INNER), lambda b, c: (b * n_chunks + c, 0)),
            pl.BlockSpec((lc, SSD_D_INNER), lambda b, c: (b * n_chunks + c, 1)),
            pl.BlockSpec((lc, 2 * SSD_GN), lambda b, c: (b * n_chunks + c, 2 * SSD_D_INNER // (2 * SSD_GN))),
            pl.BlockSpec((lc, LANES), lambda b, c: (b * n_chunks + c, 0)),
            pl.BlockSpec((1, 8, SSD_CONV_DIM), lambda b, c: (b, 0, 0)),
            pl.BlockSpec((1, SSD_PAIRS, SSD_STATE, LANES), lambda b, c: (b, 0, 0, 0)),
            const((SSD_CONV, SSD_CONV_DIM)),
            const((1, SSD_CONV_DIM)),
            const((1, LANES)),
            const((1, LANES)),
            const((1, SSD_D_INNER)),
            const((1, SSD_D_INNER)),
            const((lc, lc)),
            const((LANES, SSD_HEADS * LANES)),
        ],
        out_specs=[
            pl.BlockSpec((lc, SSD_D_INNER), lambda b, c: (b * n_chunks + c, 0)),
            pl.BlockSpec((1, SSD_PAIRS, SSD_STATE, LANES), lambda b, c: (b, 0, 0, 0)),
            pl.BlockSpec((1, 8, SSD_CONV_DIM), lambda b, c: (b, 0, 0)),
        ],
        out_shape=[
            jax.ShapeDtypeStruct((rows, SSD_D_INNER), F32),
            jax.ShapeDtypeStruct((nseq, SSD_PAIRS, SSD_STATE, LANES), F32),
            jax.ShapeDtypeStruct((nseq, 8, SSD_CONV_DIM), F32),
        ],
        scratch_shapes=[
            pltpu.VMEM((lc + 8, SSD_CONV_DIM), F32),
            pltpu.VMEM((lc, SSD_CONV_DIM), F32),
            pltpu.VMEM((lc, SSD_HEADS * LANES), F32),
            pltpu.VMEM((lc, SSD_D_INNER), F32),
        ],
        compiler_params=_params(("arbitrary", "arbitrary")),
        name="ssd_core",
    )(zxbc, zxbc, zxbc, dt_raw, conv0, ssm0, cw, cb, dtb, alog, dsk, ng, tri, expand)


def _ssd_state_to_pairs(s):
    b = s.shape[0]
    s = s.reshape(b, SSD_PAIRS, 2, SSD_HEAD_DIM, SSD_STATE)
    return jnp.transpose(s, (0, 1, 4, 2, 3)).reshape(b, SSD_PAIRS, SSD_STATE, LANES)


def _ssd_state_from_pairs(s):
    b = s.shape[0]
    s = s.reshape(b, SSD_PAIRS, SSD_STATE, 2, SSD_HEAD_DIM)
    return jnp.transpose(s, (0, 1, 3, 4, 2)).reshape(b, SSD_HEADS, SSD_HEAD_DIM, SSD_STATE)


def _sb_kernel(q_ref, k_ref, v_ref, tt_ref, o_ref, acc_sc, out_sc, *, key_block_offset):
    blk, wide = SB_BLOCK, SB_GROUP * SB_BLOCK
    q_block = pl.program_id(2) + key_block_offset
    q_pos0 = q_block * blk
    diag = q_block // SB_GROUP
    lane = lax.broadcasted_iota(jnp.int32, (blk, LANES), 1)
    q = q_ref[...] * (SB_HEAD_DIM ** -0.5 * LOG2E)
    qs = jnp.concatenate([jnp.where(lane < HALF, q, 0.0), jnp.where(lane < HALF, 0.0, q)], axis=0).astype(BF16)
    acc_sc[...] = jnp.zeros_like(acc_sc)
    out_sc[...] = jnp.zeros_like(out_sc)

    def group(s, masked):
        start = pl.multiple_of(s * wide, wide)
        kb = k_ref[pl.ds(start, wide), :].astype(BF16)
        vb = v_ref[pl.ds(start, wide), :].astype(BF16)
        z = _dot_nt(qs, kb)
        nz = -z
        u = jnp.minimum(nz, 0.0) - jnp.log2(1.0 + jnp.exp2(jnp.minimum(z, nz)))
        if masked:
            k_pos = start + lax.broadcasted_iota(jnp.int32, (2 * blk, wide), 1)
            q_pos = q_pos0 + (lax.broadcasted_iota(jnp.int32, (2 * blk, wide), 0) & (blk - 1))
            keep = k_pos < q_pos
            u = jnp.where(keep, u, 0.0)
        u_hi = u.astype(BF16).astype(F32)
        u_lo = u - u_hi
        lhs = jnp.concatenate(
            [jnp.concatenate([u_hi[:, c * blk:(c + 1) * blk], u_lo[:, c * blk:(c + 1) * blk]], axis=1)
             for c in range(SB_GROUP)], axis=0).astype(BF16)
        rt = _dot(lhs, tt_ref[...])
        acc = acc_sc[...]
        weights = [None] * SB_GROUP
        for c in reversed(range(SB_GROUP)):
            cols = slice(c * blk, (c + 1) * blk)
            rows = slice(c * 2 * blk, (c + 1) * 2 * blk)
            w = jnp.exp2(z[:, cols] + rt[rows, :blk] + acc)
            if masked:
                w = jnp.where(keep[:, cols], w, 0.0)
            weights[c] = w
            acc = acc + rt[rows, blk:]
        acc_sc[...] = acc
        out_sc[...] += _dot(jnp.concatenate(weights, axis=1).astype(BF16), vb)

    group(diag, True)

    def older(t, carry):
        group(diag - 1 - t, False)
        return carry

    lax.fori_loop(0, diag, older, 0)
    o_ref[...] = jnp.where(lane < HALF, out_sc[0:blk, :], out_sc[blk:2 * blk, :])


def _sb_attention(q_arr, q_col0, kv_arr, k_col0, v_col0, *, nseq, n_q_blocks, n_k_rows, key_block_offset):
    blk = SB_BLOCK
    assert n_k_rows % (SB_GROUP * blk) == 0
    n_pairs = SB_HEADS // 2
    ones = jnp.ones((blk, blk), F32)
    tri = (jnp.arange(blk)[:, None] >= jnp.arange(blk)[None, :]).astype(F32)
    half = jnp.concatenate([tri, ones], axis=1)
    tt = jnp.concatenate([half, half], axis=0).astype(BF16)
    return pl.pallas_call(
        functools.partial(_sb_kernel, key_block_offset=key_block_offset),
        grid=(nseq, n_pairs, n_q_blocks),
        in_specs=[
            pl.BlockSpec((blk, LANES), lambda b, p, i: (b * n_q_blocks + i, q_col0 + p)),
            pl.BlockSpec((n_k_rows, LANES), lambda b, p, i: (b, k_col0 + p)),
            pl.BlockSpec((n_k_rows, LANES), lambda b, p, i: (b, v_col0 + p)),
            pl.BlockSpec((2 * blk, 2 * blk), lambda b, p, i: (0, 0)),
        ],
        out_specs=pl.BlockSpec((blk, LANES), lambda b, p, i: (b * n_q_blocks + i, p)),
        out_shape=jax.ShapeDtypeStruct((nseq * n_q_blocks * blk, SB_HEADS * SB_HEAD_DIM), F32),
        scratch_shapes=[pltpu.VMEM((2 * blk, LANES), F32), pltpu.VMEM((2 * blk, LANES), F32)],
        compiler_params=_params(("arbitrary", "arbitrary", "arbitrary")),
        name="sb_attention",
    )(q_arr, kv_arr, kv_arr, tt)


def _swa_heads(q, kk, vv, sink_ref, valid):
    tq = q.shape[0]
    lane = lax.broadcasted_iota(jnp.int32, (tq, LANES), 1)
    lo_half = lane < HALF
    rep = SWA_Q_HEADS // SWA_KV_HEADS

    def aligned(x, g):
        blk = x[:, (g // 2) * LANES:(g // 2 + 1) * LANES]
        swapped = pltpu.roll(blk, HALF, axis=1)
        return (blk, swapped) if g % 2 == 0 else (swapped, blk)

    outs = []
    for hp in range(SWA_Q_HEADS // 2):
        qb = q[:, hp * LANES:(hp + 1) * LANES] * (SWA_HEAD_DIM ** -0.5)
        res = []
        for h2 in range(2):
            h = 2 * hp + h2
            g = h // rep
            qm = (jnp.where(lo_half, qb, 0.0) if h2 == 0 else jnp.where(lo_half, 0.0, qb)).astype(BF16)
            s = _dot_nt(qm, aligned(kk, g)[h2].astype(BF16))
            if valid is not None:
                s = jnp.where(valid, s, -jnp.inf)
            sink = sink_ref[h]
            m = jnp.maximum(jnp.max(s, axis=-1, keepdims=True), sink)
            p = jnp.exp(s - m)
            p = p * (1.0 / (jnp.sum(p, axis=-1, keepdims=True) + jnp.exp(sink - m)))
            res.append(_dot(p.astype(BF16), aligned(vv, g)[h2].astype(BF16)))
        outs.append(jnp.where(lo_half, res[0], res[1]))
    return jnp.concatenate(outs, axis=1)


def _swa_prompt_kernel(sink_ref, q_ref, kp_ref, kc_ref, vp_ref, vc_ref, o_ref):
    i = pl.program_id(1)
    blk = 2 * SWA_CHUNK
    kk = jnp.concatenate([kp_ref[...], kc_ref[...]], axis=0)
    vv = jnp.concatenate([vp_ref[...], vc_ref[...]], axis=0)
    q_chunk = lax.broadcasted_iota(jnp.int32, (blk, 2 * blk), 0) // SWA_CHUNK
    k_chunk = lax.broadcasted_iota(jnp.int32, (blk, 2 * blk), 1) // SWA_CHUNK
    valid = (k_chunk >= q_chunk) & (k_chunk <= q_chunk + 2) & ((i > 0) | (k_chunk >= 2))
    o_ref[...] = _swa_heads(q_ref[...], kk, vv, sink_ref, valid)


def _swa_prompt(qkv, sinks, *, nseq, seq_len):
    blk = 2 * SWA_CHUNK
    nb = seq_len // blk
    dq = SWA_Q_HEADS * SWA_HEAD_DIM
    dkv = SWA_KV_HEADS * SWA_HEAD_DIM
    k_col, v_col = dq // dkv, dq // dkv + 1
    prev = lambda col: (lambda b, i: (b * nb + jnp.maximum(i - 1, 0), col))
    cur = lambda col: (lambda b, i: (b * nb + i, col))
    return pl.pallas_call(
        _swa_prompt_kernel,
        grid=(nseq, nb),
        in_specs=[
            pl.BlockSpec(memory_space=pltpu.SMEM),
            pl.BlockSpec((blk, dq), cur(0)),
            pl.BlockSpec((blk, dkv), prev(k_col)),
            pl.BlockSpec((blk, dkv), cur(k_col)),
            pl.BlockSpec((blk, dkv), prev(v_col)),
            pl.BlockSpec((blk, dkv), cur(v_col)),
        ],
        out_specs=pl.BlockSpec((blk, dq), cur(0)),
        out_shape=jax.ShapeDtypeStruct((nseq * seq_len, dq), F32),
        compiler_params=_params(("arbitrary", "arbitrary")),
        name="swa_prompt",
    )(sinks, qkv, qkv, qkv, qkv, qkv)


def _swa_step_kernel(sink_ref, q_ref, k_ref, v_ref, o_ref):
    o_ref[...] = _swa_heads(q_ref[...], k_ref[0], v_ref[0], sink_ref, None)


def _swa_step(qkv, kf, vf, sinks, *, nseq, seq_len):
    dq = SWA_Q_HEADS * SWA_HEAD_DIM
    tk = kf.shape[1]
    return pl.pallas_call(
        _swa_step_kernel,
        grid=(nseq,),
        in_specs=[
            pl.BlockSpec(memory_space=pltpu.SMEM),
            pl.BlockSpec((seq_len, dq), lambda b: (b, 0)),
            pl.BlockSpec((1, tk, kf.shape[2]), lambda b: (b, 0, 0)),
            pl.BlockSpec((1, tk, vf.shape[2]), lambda b: (b, 0, 0)),
        ],
        out_specs=pl.BlockSpec((seq_len, dq), lambda b: (b, 0)),
        out_shape=jax.ShapeDtypeStruct((nseq * seq_len, dq), F32),
        compiler_params=_params(("arbitrary",)),
        name="swa_step",
    )(sinks, qkv, kf, vf)


def _pad_lanes(v, width=LANES):
    return jnp.pad(v, ((0, 0), (0, width - v.shape[-1])))


def _trunk(x, mods, is_prompt, state, weights):
    (state_ssm, state_conv, cache_sb_k, cache_sb_v, cache_swa_k, cache_swa_v) = state
    nseq, seq_len, _ = x.shape
    rows = nseq * seq_len
    x = x.reshape(rows, D_MODEL)
    if is_prompt:
        tm, tiles_per_mod = 512, seq_len // 512
    else:
        tm, tiles_per_mod = rows, 1
    row_kw = dict(tm=tm, tiles_per_mod=tiles_per_mod)
    ssm_new, conv_new, sbk_new, sbv_new, swak_new, swav_new = [], [], [], [], [], []
    for i in range(DEPTH):
        g6 = weights["norm_g"][i]
        x = _ffn(x, mods[i][0], g6, weights["ffn_w_in"][i, 0], weights["ffn_w_out"][i, 0], pre=0, post=1, **row_kw)
        kind, j = i % N_MIXERS, i // N_MIXERS
        mod = mods[i][1]
        if kind == 0:
            w_in = weights["ssd_w_in"][j]
            n_main = SSD_D_INNER + SSD_CONV_DIM
            zxbc, dt_raw = _proj(x, mod, g6, w_in[:, :n_main], _pad_lanes(w_in[:, n_main:]), pre=2, tn=PROJ_TN,
                                 **row_kw)
            lc = SSD_CHUNK
            if is_prompt:
                n_chunks, n_valid = seq_len // lc, lc
                conv0 = jnp.zeros((nseq, 8, SSD_CONV_DIM), F32)
                ssm0 = jnp.zeros((nseq, SSD_PAIRS, SSD_STATE, LANES), F32)
            else:
                n_chunks, n_valid = 1, seq_len
                pad = lambda t: jnp.pad(t.reshape(nseq, seq_len, -1), ((0, 0), (0, lc - seq_len), (0, 0))
                                        ).reshape(nseq * lc, -1)
                zxbc, dt_raw = pad(zxbc), pad(dt_raw)
                conv0 = jnp.pad(state_conv[j], ((0, 0), (8 - (SSD_CONV - 1), 0), (0, 0)))
                ssm0 = _ssd_state_to_pairs(state_ssm[j])
            y, st, conv_tail = _ssd_core(
                zxbc, dt_raw, conv0, ssm0, weights["ssd_conv_w"][j], weights["ssd_conv_b"][j][None],
                _pad_lanes(weights["ssd_dt_bias"][j][None]), _pad_lanes(weights["ssd_a_log"][j][None]),
                jnp.repeat(weights["ssd_d"][j], SSD_HEAD_DIM)[None], weights["ssd_norm_g"][j][None],
                nseq=nseq, n_chunks=n_chunks, n_valid=n_valid)
            if not is_prompt:
                y = y.reshape(nseq, lc, SSD_D_INNER)[:, :seq_len].reshape(rows, SSD_D_INNER)
            ssm_new.append(_ssd_state_from_pairs(st))
            conv_new.append(conv_tail[:, 8 - (SSD_CONV - 1):])
            w_out = weights["ssd_w_out"][j]
        elif kind == 1:
            qkv = _proj(x, mod, g6, weights["sb_w_qkv"][j], None, pre=2, tn=PROJ_TN, **row_kw)[0]
            dh = SB_HEADS * SB_HEAD_DIM
            k_new, v_new = qkv[:, dh:2 * dh], qkv[:, 2 * dh:]
            n_pairs = SB_HEADS // 2
            if is_prompt:
                y = _sb_attention(qkv, 0, qkv, n_pairs, 2 * n_pairs, nseq=nseq, n_q_blocks=seq_len // SB_BLOCK,
                                  n_k_rows=seq_len, key_block_offset=0)
            else:
                past = cache_sb_k.shape[2]
                wide = SB_GROUP * SB_BLOCK
                n_k_rows = -(-(past + seq_len) // wide) * wide
                pad_to = lambda t, n: jnp.pad(t.reshape(nseq, seq_len, dh), ((0, 0), (0, n - seq_len), (0, 0)))
                kf = jnp.concatenate([cache_sb_k[j].reshape(nseq, past, dh), pad_to(k_new, n_k_rows - past)], axis=1)
                vf = jnp.concatenate([cache_sb_v[j].reshape(nseq, past, dh), pad_to(v_new, n_k_rows - past)], axis=1)
                kv = jnp.concatenate([kf, vf], axis=2).reshape(nseq * n_k_rows, 2 * dh)
                qp = pad_to(qkv[:, :dh], SB_BLOCK).reshape(nseq * SB_BLOCK, dh)
                y = _sb_attention(qp, 0, kv, 0, n_pairs, nseq=nseq, n_q_blocks=1, n_k_rows=n_k_rows,
                                  key_block_offset=past // SB_BLOCK)
                y = y.reshape(nseq, SB_BLOCK, dh)[:, :seq_len].reshape(rows, dh)
            sbk_new.append(k_new.reshape(nseq, seq_len, SB_HEADS, SB_HEAD_DIM))
            sbv_new.append(v_new.reshape(nseq, seq_len, SB_HEADS, SB_HEAD_DIM))
            w_out = weights["sb_w_out"][j]
        else:
            dq, dkv = SWA_Q_HEADS * SWA_HEAD_DIM, SWA_KV_HEADS * SWA_HEAD_DIM
            qkv = _proj(x, mod, g6, weights["swa_w_qkv"][j], None, pre=2, tn=dq + 2 * dkv, **row_kw)[0]
            k_new = qkv[:, dq:dq + dkv].reshape(nseq, seq_len, dkv)
            v_new = qkv[:, dq + dkv:].reshape(nseq, seq_len, dkv)
            sinks = weights["swa_sinks"][j]
            if is_prompt:
                y = _swa_prompt(qkv, sinks, nseq=nseq, seq_len=seq_len)
                kf, vf = k_new, v_new
            else:
                kf = jnp.concatenate([cache_swa_k[j].reshape(nseq, -1, dkv), k_new], axis=1)
                vf = jnp.concatenate([cache_swa_v[j].reshape(nseq, -1, dkv), v_new], axis=1)
                y = _swa_step(qkv, kf, vf, sinks, nseq=nseq, seq_len=seq_len)
            n = kf.shape[1]
            swak_new.append(kf[:, n - SWA_ROWS:].reshape(nseq, SWA_ROWS, SWA_KV_HEADS, SWA_HEAD_DIM))
            swav_new.append(vf[:, n - SWA_ROWS:].reshape(nseq, SWA_ROWS, SWA_KV_HEADS, SWA_HEAD_DIM))
            w_out = weights["swa_w_out"][j]
        x = _outproj(y, x, mod, g6, w_out, post=3, **row_kw)
        x = _ffn(x, mods[i][2], g6, weights["ffn_w_in"][i, 1], weights["ffn_w_out"][i, 1], pre=4, post=5, **row_kw)
    return (x.reshape(nseq, seq_len, D_MODEL), jnp.stack(ssm_new), jnp.stack(conv_new), jnp.stack(sbk_new),
            jnp.stack(sbv_new), jnp.stack(swak_new), jnp.stack(swav_new))


def kernel(x_prompt, x_sample, c_prompt, c_sample, state_ssm, state_conv, cache_sb_k, cache_sb_v, cache_swa_k,
           cache_swa_v, ada_w, ada_b, norm_g, ffn_w_in, ffn_w_out, ssd_w_in, ssd_conv_w, ssd_conv_b, ssd_dt_bias,
           ssd_a_log, ssd_d, ssd_norm_g, ssd_w_out, sb_w_qkv, sb_w_out, swa_w_qkv, swa_sinks, swa_w_out):
    n_p, n_s = c_prompt.shape[0], c_sample.shape[0]
    s_len = x_sample.shape[1]
    c_all = jnp.concatenate([c_prompt, c_sample, jnp.zeros((16 - n_p - n_s, D_MODEL), F32)], axis=0)
    mod_all = _modulation(c_all, ada_w, ada_b).reshape(DEPTH, 16, 3, 3, 1, D_MODEL)
    mods_p = [[mod_all[i, :n_p, s] for s in range(3)] for i in range(DEPTH)]
    mods_s = [[jnp.transpose(jnp.repeat(mod_all[i, n_p:n_p + n_s, s, :, 0], s_len, axis=0), (1, 0, 2))[None]
               for s in range(3)] for i in range(DEPTH)]
    weights = dict(
        norm_g=norm_g, ffn_w_in=ffn_w_in.astype(BF16), ffn_w_out=ffn_w_out.astype(BF16),
        ssd_w_in=ssd_w_in.astype(BF16), ssd_conv_w=ssd_conv_w, ssd_conv_b=ssd_conv_b, ssd_dt_bias=ssd_dt_bias,
        ssd_a_log=ssd_a_log, ssd_d=ssd_d, ssd_norm_g=ssd_norm_g, ssd_w_out=ssd_w_out.astype(BF16),
        sb_w_qkv=sb_w_qkv.astype(BF16), sb_w_out=sb_w_out.astype(BF16), swa_w_qkv=swa_w_qkv.astype(BF16),
        swa_sinks=swa_sinks, swa_w_out=swa_w_out.astype(BF16))
    y_p, ssm_p, conv_p, sbk_p, sbv_p, swak_p, swav_p = _trunk(x_prompt, mods_p, True, (None,) * 6, weights)
    y_s, ssm_s, conv_s, sbk_s, sbv_s, swak_s, swav_s = _trunk(
        x_sample, mods_s, False, (state_ssm, state_conv, cache_sb_k, cache_sb_v, cache_swa_k, cache_swa_v), weights)
    return (y_p, y_s, ssm_p, ssm_s, conv_p, conv_s, sbk_p, sbk_s, sbv_p, sbv_s, swak_p, swak_s, swav_p, swav_s)
```

```python
import functools

import jax
import jax.numpy as jnp
from jax import lax
from jax.experimental import pallas as pl
from jax.experimental.pallas import tpu as pltpu

F32 = jnp.float32
BF16 = jnp.bfloat16

D_MODEL = 1024
DEPTH = 4
N_MIXERS = 3
NORM_EPS = 1e-6
FFN_RESID = 0.5
D_FF = 2816

SSD_D_INNER = 2048
SSD_HEAD_DIM = 64
SSD_HEADS = 32
SSD_GROUPS = 4
SSD_STATE = 128
SSD_CONV = 4
SSD_GN = SSD_GROUPS * SSD_STATE
SSD_CONV_DIM = SSD_D_INNER + 2 * SSD_GN
SSD_PAIRS = SSD_HEADS // 2
SSD_CHUNK = 128

SB_HEADS = 16
SB_HEAD_DIM = 64
SB_BLOCK = 128
SB_GROUP = 4
SB_Q_BLOCKS = 4
LOG2E = 1.4426950408889634
SB_EXIT_DROP = 152.0
SB_HIDDEN = 1e30

SWA_Q_HEADS = 16
SWA_KV_HEADS = 4
SWA_HEAD_DIM = 64
SWA_ROWS = 128
SWA_CHUNK = 64

LANES = 128
HALF = 64
VMEM_LIMIT = 48 * 1024 * 1024

FFN_CHUNK = D_FF // 2
FFN_SUB = (512, 512, 384)
PROJ_TN = 1024
ROW_TILE = 512
PROJ_ROW_TILE = 256


def _params(sem):
    return pltpu.CompilerParams(dimension_semantics=sem, vmem_limit_bytes=VMEM_LIMIT)


def _rms(x, g):
    return x * lax.rsqrt(jnp.mean(x * x, axis=-1, keepdims=True) + NORM_EPS) * g


def _silu(x):
    return x * (1.0 / (1.0 + jnp.exp(-x)))


def _softplus(x):
    return jnp.maximum(x, 0.0) + jnp.log1p(jnp.exp(-jnp.abs(x)))


def _dot(a, b):
    return jnp.dot(a, b, preferred_element_type=F32)


def _dot_nt(a, b):
    return lax.dot_general(a, b, (((1,), (1,)), ((), ())), preferred_element_type=F32)


def _split3(a):
    hi = a.astype(BF16)
    r = a - hi.astype(F32)
    mid = r.astype(BF16)
    lo = (r - mid.astype(F32)).astype(BF16)
    return hi, mid, lo


def _mod_kernel(c_ref, w_ref, b_ref, o_ref):
    a = _silu(c_ref[...]).astype(BF16)
    o_ref[0] = _dot(a, w_ref[0].astype(BF16)) + b_ref[0]


def _modulation(c_all, ada_w, ada_b):
    rows = c_all.shape[0]
    tn = 1152
    n_out = ada_w.shape[-1]
    return pl.pallas_call(
        _mod_kernel,
        grid=(DEPTH, n_out // tn),
        in_specs=[
            pl.BlockSpec((rows, D_MODEL), lambda i, j: (0, 0)),
            pl.BlockSpec((1, D_MODEL, tn), lambda i, j: (i, 0, j)),
            pl.BlockSpec((1, 1, tn), lambda i, j: (i, 0, j)),
        ],
        out_specs=pl.BlockSpec((1, rows, tn), lambda i, j: (i, 0, j)),
        out_shape=jax.ShapeDtypeStruct((DEPTH, rows, n_out), F32),
        compiler_params=_params(("arbitrary", "arbitrary")),
        name="adaln_mod",
    )(c_all, ada_w, ada_b.reshape(DEPTH, 1, n_out))


def _modulated_norm(x_ref, mod_ref, g_ref, pre):
    m = mod_ref[0]
    return _rms(x_ref[...], g_ref[pre:pre + 1, :]) * (1.0 + m[1]) + m[0]


def _ffn_kernel(xa_ref, ma_ref, xc_ref, mc_ref, g_ref, wa_ref, wb_ref, wo_ref, o_ref, h_sc, y_sc, *, pre, post,
                n_tiles):
    i, j = pl.program_id(0), pl.program_id(1)
    tm = o_ref.shape[0]
    cuts = [tm * n // len(FFN_SUB) // 16 * 16 for n in range(len(FFN_SUB))] + [tm]
    row_groups = [slice(cuts[n], cuts[n + 1]) for n in range(len(FFN_SUB))]

    def matmuls(parity, first, side_work):
        slot = 1 - parity
        h = h_sc[slot]
        start = 0
        for n, width in enumerate(FFN_SUB):
            a = _dot(h, wa_ref[:, start:start + width])
            b = _dot(h, wb_ref[:, start:start + width])
            side_work(parity, row_groups[n])
            part = _dot((_silu(a) * b).astype(BF16), wo_ref[start:start + width, :])
            if first and n == 0:
                y_sc[slot] = part
            else:
                y_sc[slot] += part
            start += width

    def finish(parity, rows):
        y = _rms(y_sc[parity, rows, :], g_ref[post:post + 1, :])
        gate = mc_ref[0][2]
        gate = gate if gate.shape[0] == 1 else gate[rows, :]
        o_ref[rows, :] = xc_ref[rows, :] + FFN_RESID * gate * y

    def prepare(parity, rows):
        m = ma_ref[0]
        shift, scale = (m[0], m[1]) if m.shape[1] == 1 else (m[0][rows, :], m[1][rows, :])
        h = _rms(xa_ref[rows, :], g_ref[pre:pre + 1, :]) * (1.0 + scale) + shift
        h_sc[parity, rows, :] = h.astype(BF16)

    @pl.when((i == 0) & (j == 0))
    def _():
        y_sc[...] = jnp.zeros_like(y_sc)

    @pl.when((i == 0) & (j == 1))
    def _():
        prepare(0, slice(0, tm))

    for parity in range(2):
        steady = (i >= 1) & (i <= n_tiles) & (i % 2 == parity)

        @pl.when(steady & (j == 0))
        def _():
            matmuls(parity, True, finish)

        @pl.when(steady & (j == 1))
        def _():
            matmuls(parity, False, prepare)

    @pl.when((i == n_tiles + 1) & (j == 0))
    def _():
        finish((n_tiles + 1) % 2, slice(0, tm))


def _ffn(x, mod, g6, w_in, w_out, *, pre, post, tm, tiles_per_mod):
    rows = x.shape[0]
    rm = mod.shape[2]
    n_tiles = rows // tm
    n_chunks = D_FF // FFN_CHUNK
    assert n_chunks == 2
    tile_a = lambda i, j: jnp.minimum(i, n_tiles - 1)
    tile_c = lambda i, j: jnp.clip(i - 2, 0, n_tiles - 1)
    return pl.pallas_call(
        functools.partial(_ffn_kernel, pre=pre, post=post, n_tiles=n_tiles),
        grid=(n_tiles + 2, n_chunks),
        in_specs=[
            pl.BlockSpec((tm, D_MODEL), lambda i, j: (tile_a(i, j), 0)),
            pl.BlockSpec((1, 3, rm, D_MODEL), lambda i, j: (tile_a(i, j) // tiles_per_mod, 0, 0, 0)),
            pl.BlockSpec((tm, D_MODEL), lambda i, j: (tile_c(i, j), 0)),
            pl.BlockSpec((1, 3, rm, D_MODEL), lambda i, j: (tile_c(i, j) // tiles_per_mod, 0, 0, 0)),
            pl.BlockSpec((6, D_MODEL), lambda i, j: (0, 0)),
            pl.BlockSpec((D_MODEL, FFN_CHUNK), lambda i, j: (0, j)),
            pl.BlockSpec((D_MODEL, FFN_CHUNK), lambda i, j: (0, j + n_chunks)),
            pl.BlockSpec((FFN_CHUNK, D_MODEL), lambda i, j: (j, 0)),
        ],
        out_specs=pl.BlockSpec((tm, D_MODEL), lambda i, j: (tile_c(i, j), 0)),
        out_shape=jax.ShapeDtypeStruct((rows, D_MODEL), F32),
        scratch_shapes=[pltpu.VMEM((2, tm, D_MODEL), BF16), pltpu.VMEM((2, tm, D_MODEL), F32)],
        compiler_params=_params(("arbitrary", "arbitrary")),
        name="ffn",
    )(x, mod, x, mod, g6, w_in, w_in, w_out)


def _proj_kernel(x_ref, mod_ref, g_ref, w_ref, *rest, pre, has_extra, n_tiles, tn):
    if has_extra:
        we_ref, o_ref, oe_ref, h_sc = rest
    else:
        o_ref, h_sc = rest
    i = pl.program_id(0)
    slot_mm, slot_next = (i + 1) % 2, i % 2
    tm = o_ref.shape[0]
    n_groups = w_ref.shape[1] // tn
    cuts = [tm * n // n_groups // 16 * 16 for n in range(n_groups)] + [tm]

    def prepare(rows):
        m = mod_ref[0]
        shift, scale = (m[0], m[1]) if m.shape[1] == 1 else (m[0][rows, :], m[1][rows, :])
        h = _rms(x_ref[rows, :], g_ref[pre:pre + 1, :]) * (1.0 + scale) + shift
        h_sc[slot_next, rows, :] = h.astype(BF16)

    def matmuls(with_prepare):
        h = h_sc[slot_mm]
        for n in range(n_groups):
            o_ref[:, n * tn:(n + 1) * tn] = _dot(h, w_ref[:, n * tn:(n + 1) * tn])
            if with_prepare:
                prepare(slice(cuts[n], cuts[n + 1]))
        if has_extra:
            oe_ref[...] = _dot(h, we_ref[...])

    @pl.when(i == 0)
    def _():
        prepare(slice(0, tm))

    @pl.when((i >= 1) & (i < n_tiles))
    def _():
        matmuls(True)

    @pl.when(i == n_tiles)
    def _():
        matmuls(False)


def _proj(x, mod, g6, w, w_extra, *, pre, tm, tn, tiles_per_mod):
    rows = x.shape[0]
    rm = mod.shape[2]
    n_out = w.shape[1]
    n_tiles = rows // tm
    has_extra = w_extra is not None
    tile_in = lambda i: jnp.minimum(i, n_tiles - 1)
    tile_out = lambda i: jnp.maximum(i - 1, 0)
    in_specs = [
        pl.BlockSpec((tm, D_MODEL), lambda i: (tile_in(i), 0)),
        pl.BlockSpec((1, 3, rm, D_MODEL), lambda i: (tile_in(i) // tiles_per_mod, 0, 0, 0)),
        pl.BlockSpec((6, D_MODEL), lambda i: (0, 0)),
        pl.BlockSpec((D_MODEL, n_out), lambda i: (0, 0)),
    ]
    out_specs = [pl.BlockSpec((tm, n_out), lambda i: (tile_out(i), 0))]
    out_shape = [jax.ShapeDtypeStruct((rows, n_out), F32)]
    args = [x, mod, g6, w]
    if has_extra:
        in_specs.append(pl.BlockSpec((D_MODEL, LANES), lambda i: (0, 0)))
        out_specs.append(pl.BlockSpec((tm, LANES), lambda i: (tile_out(i), 0)))
        out_shape.append(jax.ShapeDtypeStruct((rows, LANES), F32))
        args.append(w_extra)
    return pl.pallas_call(
        functools.partial(_proj_kernel, pre=pre, has_extra=has_extra, n_tiles=n_tiles, tn=tn),
        grid=(n_tiles + 1,),
        in_specs=in_specs,
        out_specs=out_specs,
        out_shape=out_shape,
        scratch_shapes=[pltpu.VMEM((2, tm, D_MODEL), BF16)],
        compiler_params=_params(("arbitrary",)),
        name="mixer_in_proj",
    )(*args)


def _outproj_kernel(y_ref, x_ref, mod_ref, g_ref, w_ref, o_ref, *, post):
    y = _dot(y_ref[...].astype(BF16), w_ref[...])
    o_ref[...] = x_ref[...] + mod_ref[0][2] * _rms(y, g_ref[post:post + 1, :])


def _outproj(y, x, mod, g6, w, *, post, tm, tiles_per_mod):
    rows = x.shape[0]
    rm = mod.shape[2]
    k_in = w.shape[0]
    return pl.pallas_call(
        functools.partial(_outproj_kernel, post=post),
        grid=(rows // tm,),
        in_specs=[
            pl.BlockSpec((tm, k_in), lambda i: (i, 0)),
            pl.BlockSpec((tm, D_MODEL), lambda i: (i, 0)),
            pl.BlockSpec((1, 3, rm, D_MODEL), lambda i: (i // tiles_per_mod, 0, 0, 0)),
            pl.BlockSpec((6, D_MODEL), lambda i: (0, 0)),
            pl.BlockSpec((k_in, D_MODEL), lambda i: (0, 0)),
        ],
        out_specs=pl.BlockSpec((tm, D_MODEL), lambda i: (i, 0)),
        out_shape=jax.ShapeDtypeStruct((rows, D_MODEL), F32),
        compiler_params=_params(("arbitrary",)),
        name="mixer_out_proj",
    )(y, x, mod, g6, w)


def _ssd_kernel(z_ref, x_ref, bc_ref, dt_ref, conv0_ref, ssm0_ref, cw_ref, cb_ref, dtb_ref, alog_ref, dsk_ref,
                ng_ref, tri_ref, exp_ref, y_ref, st_ref, convo_ref, xext_sc, act_sc, ax_sc, y_sc, *, n_valid):
    lc = SSD_CHUNK
    c = pl.program_id(1)

    @pl.when(c == 0)
    def _():
        xext_sc[0:8, :] = conv0_ref[0]
        st_ref[0] = ssm0_ref[0]

    xext_sc[8:8 + lc, 0:SSD_D_INNER] = x_ref[...]
    xext_sc[8:8 + lc, SSD_D_INNER:SSD_CONV_DIM] = bc_ref[...]
    cwid = 256
    for c0 in range(0, SSD_CONV_DIM, cwid):
        cols = slice(c0, c0 + cwid)
        conv = cb_ref[:, cols] + xext_sc[8:8 + lc, cols] * cw_ref[3:4, cols]
        for k in range(1, SSD_CONV):
            conv = conv + xext_sc[8 - k:8 - k + lc, cols] * cw_ref[3 - k:4 - k, cols]
        act_sc[:, cols] = _silu(conv)
    new_tail = xext_sc[n_valid:n_valid + 8, :]
    xext_sc[0:8, :] = new_tail
    convo_ref[0] = new_tail

    dt = _softplus(dt_ref[...] + dtb_ref[...])
    if n_valid < lc:
        dt = jnp.where(lax.broadcasted_iota(jnp.int32, (lc, LANES), 0) < n_valid, dt, 0.0)
    a = dt * (-jnp.exp(alog_ref[...]))
    tri = tri_ref[...]
    acum = sum(_dot(tri, part) for part in _split3(a))
    a_last = acum[lc - 1:lc, :]
    w_end = jnp.exp(a_last - acum) * dt
    acum_t = acum.T
    dt_t = dt.T
    w_t = w_end.T
    ax_sc[...] = sum(_dot(part, exp_ref[...]) for part in _split3(acum))

    row = lax.broadcasted_iota(jnp.int32, (lc, lc), 0)
    col = lax.broadcasted_iota(jnp.int32, (lc, lc), 1)
    causal = col <= row
    lane = lax.broadcasted_iota(jnp.int32, (lc, LANES), 1)
    lo_half = lane < HALF

    for g in range(SSD_GROUPS):
        b_g = act_sc[:, SSD_D_INNER + g * SSD_STATE:SSD_D_INNER + (g + 1) * SSD_STATE]
        c_g = act_sc[:, SSD_D_INNER + SSD_GN + g * SSD_STATE:SSD_D_INNER + SSD_GN + (g + 1) * SSD_STATE]
        cb = _dot_nt(c_g.astype(BF16), b_g.astype(BF16))
        b_gt = b_g.T
        for k in range(g * SSD_PAIRS // SSD_GROUPS, (g + 1) * SSD_PAIRS // SSD_GROUPS):
            xp = act_sc[:, k * LANES:(k + 1) * LANES]
            x_lo = jnp.where(lo_half, xp, 0.0)
            x_hi = jnp.where(lo_half, 0.0, xp)
            hp = st_ref[0, k]
            h_lo = jnp.where(lo_half, hp, 0.0)
            h_hi = jnp.where(lo_half, 0.0, hp)
            lhs, bw = [], []
            for h2 in range(2):
                h = 2 * k + h2
                ah = ax_sc[:, h * LANES:(h + 1) * LANES]
                seg = jnp.where(causal, ah - acum_t[h:h + 1, :], -1e30)
                lhs.append(cb * jnp.exp(seg) * dt_t[h:h + 1, :])
                bw.append(b_gt * w_t[h:h + 1, :])
            for h2 in range(2):
                h = 2 * k + h2
                lhs.append(c_g * jnp.exp(ax_sc[:, h * LANES:(h + 1) * LANES]))
            lhs = jnp.concatenate(lhs, axis=1).astype(BF16)
            rhs = jnp.concatenate([x_lo, x_hi, h_lo, h_hi], axis=0).astype(BF16)
            y_pair = _dot(lhs, rhs) + dsk_ref[:, k * LANES:(k + 1) * LANES] * xp
            y_sc[:, k * LANES:(k + 1) * LANES] = y_pair
            s_new = _dot(jnp.concatenate(bw, axis=1).astype(BF16),
                         jnp.concatenate([x_lo, x_hi], axis=0).astype(BF16))
            decay = jnp.where(lo_half[0:1, :],
                              jnp.exp(ax_sc[lc - 1:lc, (2 * k) * LANES:(2 * k + 1) * LANES]),
                              jnp.exp(ax_sc[lc - 1:lc, (2 * k + 1) * LANES:(2 * k + 2) * LANES]))
            st_ref[0, k] = hp * decay + s_new

    yz = y_sc[...] * _silu(z_ref[...])
    y_ref[...] = _rms(yz, ng_ref[...])


def _ssd_core(zxbc, dt_raw, conv0, ssm0, cw, cb, dtb, alog, dsk, ng, *, nseq, n_chunks, n_valid):
    lc = SSD_CHUNK
    rows = zxbc.shape[0]
    tri = (jnp.arange(lc)[:, None] >= jnp.arange(lc)[None, :]).astype(BF16)
    expand = (jnp.arange(SSD_HEADS * LANES)[None, :] // LANES == jnp.arange(LANES)[:, None]).astype(BF16)
    const = lambda shape: pl.BlockSpec(shape, lambda b, c: (0,) * len(shape))
    return pl.pallas_call(
        functools.partial(_ssd_kernel, n_valid=n_valid),
        grid=(nseq, n_chunks),
        in_specs=[
            pl.BlockSpec((lc, SSD_D_INNER), lambda b, c: (b * n_chunks + c, 0)),
            pl.BlockSpec((lc, SSD_D_INNER), lambda b, c: (b * n_chunks + c, 1)),
            pl.BlockSpec((lc, 2 * SSD_GN), lambda b, c: (b * n_chunks + c, 2 * SSD_D_INNER // (2 * SSD_GN))),
            pl.BlockSpec((lc, LANES), lambda b, c: (b * n_chunks + c, 0)),
            pl.BlockSpec((1, 8, SSD_CONV_DIM), lambda b, c: (b, 0, 0)),
            pl.BlockSpec((1, SSD_PAIRS, SSD_STATE, LANES), lambda b, c: (b, 0, 0, 0)),
            const((SSD_CONV, SSD_CONV_DIM)),
            const((1, SSD_CONV_DIM)),
            const((1, LANES)),
            const((1, LANES)),
            const((1, SSD_D_INNER)),
            const((1, SSD_D_INNER)),
            const((lc, lc)),
            const((LANES, SSD_HEADS * LANES)),
        ],
        out_specs=[
            pl.BlockSpec((lc, SSD_D_INNER), lambda b, c: (b * n_chunks + c, 0)),
            pl.BlockSpec((1, SSD_PAIRS, SSD_STATE, LANES), lambda b, c: (b, 0, 0, 0)),
            pl.BlockSpec((1, 8, SSD_CONV_DIM), lambda b, c: (b, 0, 0)),
        ],
        out_shape=[
            jax.ShapeDtypeStruct((rows, SSD_D_INNER), F32),
            jax.ShapeDtypeStruct((nseq, SSD_PAIRS, SSD_STATE, LANES), F32),
            jax.ShapeDtypeStruct((nseq, 8, SSD_CONV_DIM), F32),
        ],
        scratch_shapes=[
            pltpu.VMEM((lc + 8, SSD_CONV_DIM), F32),
            pltpu.VMEM((lc, SSD_CONV_DIM), F32),
            pltpu.VMEM((lc, SSD_HEADS * LANES), F32),
            pltpu.VMEM((lc, SSD_D_INNER), F32),
        ],
        compiler_params=_params(("arbitrary", "arbitrary")),
        name="ssd_core",
    )(zxbc, zxbc, zxbc, dt_raw, conv0, ssm0, cw, cb, dtb, alog, dsk, ng, tri, expand)


def _ssd_state_to_pairs(s):
    b = s.shape[0]
    s = s.reshape(b, SSD_PAIRS, 2, SSD_HEAD_DIM, SSD_STATE)
    return jnp.transpose(s, (0, 1, 4, 2, 3)).reshape(b, SSD_PAIRS, SSD_STATE, LANES)


def _ssd_state_from_pairs(s):
    b = s.shape[0]
    s = s.reshape(b, SSD_PAIRS, SSD_STATE, 2, SSD_HEAD_DIM)
    return jnp.transpose(s, (0, 1, 3, 4, 2)).reshape(b, SSD_HEADS, SSD_HEAD_DIM, SSD_STATE)


def _sb_kernel(q_ref, k_ref, v_ref, tt_ref, o_ref, drop_sc, out_sc, *, key_block_offset, n_qb):
    blk, wide = SB_BLOCK, SB_GROUP * SB_BLOCK
    newest_first = tuple(reversed(range(SB_GROUP)))
    lane = lax.broadcasted_iota(jnp.int32, (blk, LANES), 1)
    key_off = lax.broadcasted_iota(jnp.int32, (2 * blk, wide), 1)
    row_q = lax.broadcasted_iota(jnp.int32, (2 * blk, LANES), 0) & (blk - 1)
    first_block = pl.program_id(2) * n_qb + key_block_offset
    q_pos, qs = [], []
    for r in range(n_qb):
        q_pos.append((first_block + r) * blk + row_q)
        q = q_ref[r * blk:(r + 1) * blk, :] * (SB_HEAD_DIM ** -0.5 * LOG2E)
        qs.append(jnp.concatenate([jnp.where(lane < HALF, q, 0.0), jnp.where(lane < HALF, 0.0, q)],
                                  axis=0).astype(BF16))
        drop_sc[r] = jnp.zeros((2 * blk, LANES), F32)
        out_sc[r] = jnp.zeros((2 * blk, LANES), F32)

    def scores(r, hi):
        start = pl.multiple_of(jnp.maximum(hi - wide, 0), blk)
        kb = k_ref[pl.ds(start, wide), :].astype(BF16)
        vb = v_ref[pl.ds(start, wide), :].astype(BF16)
        visible = jnp.minimum(q_pos[r], hi) - start
        keep = key_off < jnp.concatenate([visible] * SB_GROUP, axis=1)
        return dict(start=start, vb=vb, z=jnp.where(keep, _dot_nt(qs[r], kb), -SB_HIDDEN))

    def softplus_split(st):
        z = st["z"]
        neg_abs = lax.bitcast_convert_type(lax.bitcast_convert_type(z, jnp.int32) | jnp.int32(-2 ** 31), F32)
        s = jnp.maximum(z, 0.0) + jnp.log2(1.0 + jnp.exp2(neg_abs))
        s_hi = s.astype(BF16)
        s_lo = (s - s_hi.astype(F32)).astype(BF16)
        st["lhs"] = jnp.concatenate(
            [jnp.concatenate([s_hi[:, c * blk:(c + 1) * blk], s_lo[:, c * blk:(c + 1) * blk]], axis=1)
             for c in newest_first], axis=0)

    def cumsum(st):
        st["rt"] = _dot(st.pop("lhs"), tt_ref[...])

    def weights(r, st):
        z, rt, drop = st.pop("z"), st.pop("rt"), drop_sc[r]
        w = [None] * SB_GROUP
        for i, c in enumerate(newest_first):
            rows = slice(i * 2 * blk, (i + 1) * 2 * blk)
            w[c] = jnp.exp2(z[:, c * blk:(c + 1) * blk] - rt[rows, :blk] - drop)
            drop = drop + rt[rows, blk:]
        drop_sc[r] = drop
        st["w"] = jnp.concatenate(w, axis=1).astype(BF16)

    def accumulate(r, st):
        out_sc[r] += _dot(st.pop("w"), st.pop("vb"))

    states = [None] * n_qb
    for t in range(n_qb + 2):
        if t < n_qb:
            states[t] = scores(t, (first_block + t + 1) * blk)
        if 0 <= t - 1 < n_qb:
            softplus_split(states[t - 1])
            cumsum(states[t - 1])
        if 0 <= t - 2 < n_qb:
            weights(t - 2, states[t - 2])
            accumulate(t - 2, states[t - 2])

    def older_group(r, hi):
        st = scores(r, hi)
        softplus_split(st)
        cumsum(st)
        weights(r, st)
        accumulate(r, st)
        return st["start"]

    least = functools.reduce(jnp.minimum, [jnp.where(states[r]["start"] > 0, drop_sc[r], SB_EXIT_DROP)
                                           for r in range(n_qb)])

    @pl.when(jnp.min(least) < SB_EXIT_DROP)
    def _():
        for r in range(n_qb):
            lax.while_loop(lambda hi, r=r: (hi > 0) & (jnp.min(drop_sc[r]) < SB_EXIT_DROP),
                           functools.partial(older_group, r), states[r]["start"])

    for r in range(n_qb):
        o_ref[r * blk:(r + 1) * blk, :] = jnp.where(lane < HALF, out_sc[r, 0:blk, :], out_sc[r, blk:2 * blk, :])


def _sb_attention(q_arr, q_col0, kv_arr, k_col0, v_col0, *, nseq, n_q_blocks, n_k_rows, key_block_offset):
    blk = SB_BLOCK
    assert n_k_rows % blk == 0 and n_k_rows >= SB_GROUP * blk
    n_qb = SB_Q_BLOCKS if n_q_blocks % SB_Q_BLOCKS == 0 else 1
    n_steps = n_q_blocks // n_qb
    n_pairs = SB_HEADS // 2
    ones = jnp.ones((blk, blk), F32)
    tri = (jnp.arange(blk)[:, None] >= jnp.arange(blk)[None, :]).astype(F32)
    half = jnp.concatenate([tri, ones], axis=1)
    tt = jnp.concatenate([half, half], axis=0).astype(BF16)
    return pl.pallas_call(
        functools.partial(_sb_kernel, key_block_offset=key_block_offset, n_qb=n_qb),
        grid=(nseq, n_pairs, n_steps),
        in_specs=[
            pl.BlockSpec((n_qb * blk, LANES), lambda b, p, i: (b * n_steps + i, q_col0 + p)),
            pl.BlockSpec((n_k_rows, LANES), lambda b, p, i: (b, k_col0 + p)),
            pl.BlockSpec((n_k_rows, LANES), lambda b, p, i: (b, v_col0 + p)),
            pl.BlockSpec((2 * blk, 2 * blk), lambda b, p, i: (0, 0)),
        ],
        out_specs=pl.BlockSpec((n_qb * blk, LANES), lambda b, p, i: (b * n_steps + i, p)),
        out_shape=jax.ShapeDtypeStruct((nseq * n_q_blocks * blk, SB_HEADS * SB_HEAD_DIM), F32),
        scratch_shapes=[pltpu.VMEM((n_qb, 2 * blk, LANES), F32), pltpu.VMEM((n_qb, 2 * blk, LANES), F32)],
        compiler_params=_params(("arbitrary", "arbitrary", "arbitrary")),
        name="sb_attention",
    )(q_arr, kv_arr, kv_arr, tt)


def _swa_heads(q, kk, vv, sink_ref, valid):
    tq = q.shape[0]
    lane = lax.broadcasted_iota(jnp.int32, (tq, LANES), 1)
    lo_half = lane < HALF
    rep = SWA_Q_HEADS // SWA_KV_HEADS

    def aligned(x, g):
        blk = x[:, (g // 2) * LANES:(g // 2 + 1) * LANES]
        swapped = pltpu.roll(blk, HALF, axis=1)
        return (blk, swapped) if g % 2 == 0 else (swapped, blk)

    outs = []
    for hp in range(SWA_Q_HEADS // 2):
        qb = q[:, hp * LANES:(hp + 1) * LANES] * (SWA_HEAD_DIM ** -0.5)
        res = []
        for h2 in range(2):
            h = 2 * hp + h2
            g = h // rep
            qm = (jnp.where(lo_half, qb, 0.0) if h2 == 0 else jnp.where(lo_half, 0.0, qb)).astype(BF16)
            s = _dot_nt(qm, aligned(kk, g)[h2].astype(BF16))
            if valid is not None:
                s = jnp.where(valid, s, -jnp.inf)
            sink = sink_ref[h]
            m = jnp.maximum(jnp.max(s, axis=-1, keepdims=True), sink)
            p = jnp.exp(s - m)
            p = p * (1.0 / (jnp.sum(p, axis=-1, keepdims=True) + jnp.exp(sink - m)))
            res.append(_dot(p.astype(BF16), aligned(vv, g)[h2].astype(BF16)))
        outs.append(jnp.where(lo_half, res[0], res[1]))
    return jnp.concatenate(outs, axis=1)


def _swa_prompt_kernel(sink_ref, q_ref, kp_ref, kc_ref, vp_ref, vc_ref, o_ref):
    i = pl.program_id(1)
    blk = 2 * SWA_CHUNK
    kk = jnp.concatenate([kp_ref[...], kc_ref[...]], axis=0)
    vv = jnp.concatenate([vp_ref[...], vc_ref[...]], axis=0)
    q_chunk = lax.broadcasted_iota(jnp.int32, (blk, 2 * blk), 0) // SWA_CHUNK
    k_chunk = lax.broadcasted_iota(jnp.int32, (blk, 2 * blk), 1) // SWA_CHUNK
    valid = (k_chunk >= q_chunk) & (k_chunk <= q_chunk + 2) & ((i > 0) | (k_chunk >= 2))
    o_ref[...] = _swa_heads(q_ref[...], kk, vv, sink_ref, valid)


def _swa_prompt(qkv, sinks, *, nseq, seq_len):
    blk = 2 * SWA_CHUNK
    nb = seq_len // blk
    dq = SWA_Q_HEADS * SWA_HEAD_DIM
    dkv = SWA_KV_HEADS * SWA_HEAD_DIM
    k_col, v_col = dq // dkv, dq // dkv + 1
    prev = lambda col: (lambda b, i: (b * nb + jnp.maximum(i - 1, 0), col))
    cur = lambda col: (lambda b, i: (b * nb + i, col))
    return pl.pallas_call(
        _swa_prompt_kernel,
        grid=(nseq, nb),
        in_specs=[
            pl.BlockSpec(memory_space=pltpu.SMEM),
            pl.BlockSpec((blk, dq), cur(0)),
            pl.BlockSpec((blk, dkv), prev(k_col)),
            pl.BlockSpec((blk, dkv), cur(k_col)),
            pl.BlockSpec((blk, dkv), prev(v_col)),
            pl.BlockSpec((blk, dkv), cur(v_col)),
        ],
        out_specs=pl.BlockSpec((blk, dq), cur(0)),
        out_shape=jax.ShapeDtypeStruct((nseq * seq_len, dq), F32),
        compiler_params=_params(("arbitrary", "arbitrary")),
        name="swa_prompt",
    )(sinks, qkv, qkv, qkv, qkv, qkv)


def _swa_step_kernel(sink_ref, q_ref, k_ref, v_ref, o_ref):
    o_ref[...] = _swa_heads(q_ref[...], k_ref[0], v_ref[0], sink_ref, None)


def _swa_step(qkv, kf, vf, sinks, *, nseq, seq_len):
    dq = SWA_Q_HEADS * SWA_HEAD_DIM
    tk = kf.shape[1]
    return pl.pallas_call(
        _swa_step_kernel,
        grid=(nseq,),
        in_specs=[
            pl.BlockSpec(memory_space=pltpu.SMEM),
            pl.BlockSpec((seq_len, dq), lambda b: (b, 0)),
            pl.BlockSpec((1, tk, kf.shape[2]), lambda b: (b, 0, 0)),
            pl.BlockSpec((1, tk, vf.shape[2]), lambda b: (b, 0, 0)),
        ],
        out_specs=pl.BlockSpec((seq_len, dq), lambda b: (b, 0)),
        out_shape=jax.ShapeDtypeStruct((nseq * seq_len, dq), F32),
        compiler_params=_params(("arbitrary",)),
        name="swa_step",
    )(sinks, qkv, kf, vf)


def _pad_lanes(v, width=LANES):
    return jnp.pad(v, ((0, 0), (0, width - v.shape[-1])))


def _trunk(x, mods, is_prompt, state, weights):
    (state_ssm, state_conv, cache_sb_k, cache_sb_v, cache_swa_k, cache_swa_v) = state
    nseq, seq_len, _ = x.shape
    rows = nseq * seq_len
    x = x.reshape(rows, D_MODEL)
    if is_prompt:
        row_kw = dict(tm=ROW_TILE, tiles_per_mod=seq_len // ROW_TILE)
        proj_kw = dict(tm=PROJ_ROW_TILE, tiles_per_mod=seq_len // PROJ_ROW_TILE)
    else:
        row_kw = proj_kw = dict(tm=rows, tiles_per_mod=1)
    ssm_new, conv_new, sbk_new, sbv_new, swak_new, swav_new = [], [], [], [], [], []
    for i in range(DEPTH):
        g6 = weights["norm_g"][i]
        x = _ffn(x, mods[i][0], g6, weights["ffn_w_in"][i, 0], weights["ffn_w_out"][i, 0], pre=0, post=1, **row_kw)
        kind, j = i % N_MIXERS, i // N_MIXERS
        mod = mods[i][1]
        if kind == 0:
            w_in = weights["ssd_w_in"][j]
            n_main = SSD_D_INNER + SSD_CONV_DIM
            zxbc, dt_raw = _proj(x, mod, g6, w_in[:, :n_main], _pad_lanes(w_in[:, n_main:]), pre=2, tn=PROJ_TN,
                                 **proj_kw)
            lc = SSD_CHUNK
            if is_prompt:
                n_chunks, n_valid = seq_len // lc, lc
                conv0 = jnp.zeros((nseq, 8, SSD_CONV_DIM), F32)
                ssm0 = jnp.zeros((nseq, SSD_PAIRS, SSD_STATE, LANES), F32)
            else:
                n_chunks, n_valid = 1, seq_len
                pad = lambda t: jnp.pad(t.reshape(nseq, seq_len, -1), ((0, 0), (0, lc - seq_len), (0, 0))
                                        ).reshape(nseq * lc, -1)
                zxbc, dt_raw = pad(zxbc), pad(dt_raw)
                conv0 = jnp.pad(state_conv[j], ((0, 0), (8 - (SSD_CONV - 1), 0), (0, 0)))
                ssm0 = _ssd_state_to_pairs(state_ssm[j])
            y, st, conv_tail = _ssd_core(
                zxbc, dt_raw, conv0, ssm0, weights["ssd_conv_w"][j], weights["ssd_conv_b"][j][None],
                _pad_lanes(weights["ssd_dt_bias"][j][None]), _pad_lanes(weights["ssd_a_log"][j][None]),
                jnp.repeat(weights["ssd_d"][j], SSD_HEAD_DIM)[None], weights["ssd_norm_g"][j][None],
                nseq=nseq, n_chunks=n_chunks, n_valid=n_valid)
            if not is_prompt:
                y = y.reshape(nseq, lc, SSD_D_INNER)[:, :seq_len].reshape(rows, SSD_D_INNER)
            ssm_new.append(_ssd_state_from_pairs(st))
            conv_new.append(conv_tail[:, 8 - (SSD_CONV - 1):])
            w_out = weights["ssd_w_out"][j]
        elif kind == 1:
            qkv = _proj(x, mod, g6, weights["sb_w_qkv"][j], None, pre=2, tn=PROJ_TN, **proj_kw)[0]
            dh = SB_HEADS * SB_HEAD_DIM
            k_new, v_new = qkv[:, dh:2 * dh], qkv[:, 2 * dh:]
            n_pairs = SB_HEADS // 2
            if is_prompt:
                y = _sb_attention(qkv, 0, qkv, n_pairs, 2 * n_pairs, nseq=nseq, n_q_blocks=seq_len // SB_BLOCK,
                                  n_k_rows=seq_len, key_block_offset=0)
            else:
                past = cache_sb_k.shape[2]
                n_k_rows = -(-(past + seq_len) // SB_BLOCK) * SB_BLOCK
                pad_to = lambda t, n: jnp.pad(t.reshape(nseq, seq_len, dh), ((0, 0), (0, n - seq_len), (0, 0)))
                kf = jnp.concatenate([cache_sb_k[j].reshape(nseq, past, dh), pad_to(k_new, n_k_rows - past)], axis=1)
                vf = jnp.concatenate([cache_sb_v[j].reshape(nseq, past, dh), pad_to(v_new, n_k_rows - past)], axis=1)
                kv = jnp.concatenate([kf, vf], axis=2).reshape(nseq * n_k_rows, 2 * dh)
                qp = pad_to(qkv[:, :dh], SB_BLOCK).reshape(nseq * SB_BLOCK, dh)
                y = _sb_attention(qp, 0, kv, 0, n_pairs, nseq=nseq, n_q_blocks=1, n_k_rows=n_k_rows,
                                  key_block_offset=past // SB_BLOCK)
                y = y.reshape(nseq, SB_BLOCK, dh)[:, :seq_len].reshape(rows, dh)
            sbk_new.append(k_new.reshape(nseq, seq_len, SB_HEADS, SB_HEAD_DIM))
            sbv_new.append(v_new.reshape(nseq, seq_len, SB_HEADS, SB_HEAD_DIM))
            w_out = weights["sb_w_out"][j]
        else:
            dq, dkv = SWA_Q_HEADS * SWA_HEAD_DIM, SWA_KV_HEADS * SWA_HEAD_DIM
            qkv = _proj(x, mod, g6, weights["swa_w_qkv"][j], None, pre=2, tn=(dq + 2 * dkv) // 3, **proj_kw)[0]
            k_new = qkv[:, dq:dq + dkv].reshape(nseq, seq_len, dkv)
            v_new = qkv[:, dq + dkv:].reshape(nseq, seq_len, dkv)
            sinks = weights["swa_sinks"][j]
            if is_prompt:
                y = _swa_prompt(qkv, sinks, nseq=nseq, seq_len=seq_len)
                kf, vf = k_new, v_new
            else:
                kf = jnp.concatenate([cache_swa_k[j].reshape(nseq, -1, dkv), k_new], axis=1)
                vf = jnp.concatenate([cache_swa_v[j].reshape(nseq, -1, dkv), v_new], axis=1)
                y = _swa_step(qkv, kf, vf, sinks, nseq=nseq, seq_len=seq_len)
            n = kf.shape[1]
            swak_new.append(kf[:, n - SWA_ROWS:].reshape(nseq, SWA_ROWS, SWA_KV_HEADS, SWA_HEAD_DIM))
            swav_new.append(vf[:, n - SWA_ROWS:].reshape(nseq, SWA_ROWS, SWA_KV_HEADS, SWA_HEAD_DIM))
            w_out = weights["swa_w_out"][j]
        x = _outproj(y, x, mod, g6, w_out, post=3, **row_kw)
        x = _ffn(x, mods[i][2], g6, weights["ffn_w_in"][i, 1], weights["ffn_w_out"][i, 1], pre=4, post=5, **row_kw)
    return (x.reshape(nseq, seq_len, D_MODEL), jnp.stack(ssm_new), jnp.stack(conv_new), jnp.stack(sbk_new),
            jnp.stack(sbv_new), jnp.stack(swak_new), jnp.stack(swav_new))


def kernel(x_prompt, x_sample, c_prompt, c_sample, state_ssm, state_conv, cache_sb_k, cache_sb_v, cache_swa_k,
           cache_swa_v, ada_w, ada_b, norm_g, ffn_w_in, ffn_w_out, ssd_w_in, ssd_conv_w, ssd_conv_b, ssd_dt_bias,
           ssd_a_log, ssd_d, ssd_norm_g, ssd_w_out, sb_w_qkv, sb_w_out, swa_w_qkv, swa_sinks, swa_w_out):
    n_p, n_s = c_prompt.shape[0], c_sample.shape[0]
    s_len = x_sample.shape[1]
    c_all = jnp.concatenate([c_prompt, c_sample, jnp.zeros((16 - n_p - n_s, D_MODEL), F32)], axis=0)
    mod_all = _modulation(c_all, ada_w, ada_b).reshape(DEPTH, 16, 3, 3, 1, D_MODEL)
    mods_p = [[mod_all[i, :n_p, s] for s in range(3)] for i in range(DEPTH)]
    mods_s = [[jnp.transpose(jnp.repeat(mod_all[i, n_p:n_p + n_s, s, :, 0], s_len, axis=0), (1, 0, 2))[None]
               for s in range(3)] for i in range(DEPTH)]
    weights = dict(
        norm_g=norm_g, ffn_w_in=ffn_w_in.astype(BF16), ffn_w_out=ffn_w_out.astype(BF16),
        ssd_w_in=ssd_w_in.astype(BF16), ssd_conv_w=ssd_conv_w, ssd_conv_b=ssd_conv_b, ssd_dt_bias=ssd_dt_bias,
        ssd_a_log=ssd_a_log, ssd_d=ssd_d, ssd_norm_g=ssd_norm_g, ssd_w_out=ssd_w_out.astype(BF16),
        sb_w_qkv=sb_w_qkv.astype(BF16), sb_w_out=sb_w_out.astype(BF16), swa_w_qkv=swa_w_qkv.astype(BF16),
        swa_sinks=swa_sinks, swa_w_out=swa_w_out.astype(BF16))
    y_p, ssm_p, conv_p, sbk_p, sbv_p, swak_p, swav_p = _trunk(x_prompt, mods_p, True, (None,) * 6, weights)
    y_s, ssm_s, conv_s, sbk_s, sbv_s, swak_s, swav_s = _trunk(
        x_sample, mods_s, False, (state_ssm, state_conv, cache_sb_k, cache_sb_v, cache_swa_k, cache_swa_v), weights)
    return (y_p, y_s, ssm_p, ssm_s, conv_p, conv_s, sbk_p, sbk_s, sbv_p, sbv_s, swak_p, swak_s, swav_p, swav_s)
```
